```python
import math
import jax
import jax.numpy as jnp
from jax import lax
import numpy as np

D_MODEL = 1024
BATCH = 4
SEQ = 8192
DEPTH = 2

CHUNK = 64
N_A_LAYERS = DEPTH // 2
N_B_LAYERS = DEPTH - N_A_LAYERS
DN_HEADS = 8
DN_DK = 128
DN_DV = 128
DN_QK_WIDTH = DN_HEADS * DN_DK
DN_V_WIDTH = DN_HEADS * DN_DV
CONV_WIDTH = 4
DN_IN_WIDTH = 2 * DN_QK_WIDTH + 2 * DN_V_WIDTH + 2 * DN_HEADS
ATT_HEADS = 16
ATT_DH = 64
ATT_WIDTH = ATT_HEADS * ATT_DH
LEFT_CHUNKS = 8
BAND = (LEFT_CHUNKS + 1) * CHUNK
MAX_REL = 256
REL_BIAS_SIZE = CHUNK + MAX_REL
N_EXPERTS = 256
TOP_K = 8
N_GROUPS = 8
TOPK_GROUPS = 4
D_EXPERT = 256
D_SHARED = 256
ROUTED_SCALE = 2.5
EXPERT_BLOCK = 128
DN_ALPHA = (2 * DEPTH) ** 0.25
DN_BETA = (8 * DEPTH) ** -0.25
LN_EPS = 1e-5
NORM_EPS = 1e-6

kernel_name = "yoco_deltanet_chunkattn_moe_deepnorm"


def layer_norm(x, g, b):
    xf = x.astype(jnp.float32)
    mu = jnp.mean(xf, axis=-1, keepdims=True)
    var = jnp.mean(jnp.square(xf - mu), axis=-1, keepdims=True)
    return ((xf - mu) * lax.rsqrt(var + LN_EPS) * g + b).astype(x.dtype)


def l2norm(x):
    xf = x.astype(jnp.float32)
    return xf * lax.rsqrt(jnp.sum(xf * xf, axis=-1, keepdims=True) + NORM_EPS)


def causal_short_conv(x, w):
    s = x.shape[1]
    xp = jnp.pad(x, ((0, 0), (CONV_WIDTH - 1, 0), (0, 0)))
    y = xp[:, 0:s] * w[0]
    for j in range(1, CONV_WIDTH):
        y = y + xp[:, j:j + s] * w[j]
    return jax.nn.silu(y)


def to_chunks(t):
    b, s, h = t.shape[:3]
    nc = s // CHUNK
    if t.ndim == 4:
        return t.reshape(b, nc, CHUNK, h, t.shape[3]).transpose(1, 0, 3, 2, 4)
    return t.reshape(b, nc, CHUNK, h).transpose(1, 0, 3, 2)


def chunk_gated_delta_rule(q, k, v, g, beta):
    bsz, s, h, dk = q.shape
    dv = v.shape[-1]
    q = to_chunks(q * (dk ** -0.5))
    k = to_chunks(k)
    v = to_chunks(v)
    g = to_chunks(g)
    beta = to_chunks(beta)
    gc = jnp.cumsum(g, axis=-1)
    idx = jnp.arange(CHUNK)
    incl = idx[:, None] >= idx[None, :]
    strict = idx[:, None] > idx[None, :]
    decay = jnp.exp(jnp.where(incl, gc[..., :, None] - gc[..., None, :], -jnp.inf))
    kb = k * beta[..., None]
    a_mat = jnp.where(strict, jnp.einsum('nbhid,nbhjd->nbhij', kb, k) * decay, 0.0)
    m = a_mat + jnp.eye(CHUNK, dtype=a_mat.dtype)
    rhs = jnp.concatenate([v * beta[..., None], kb * jnp.exp(gc)[..., None]], axis=-1)
    sol = lax.linalg.triangular_solve(m, rhs, left_side=True, lower=True, unit_diagonal=True)
    u = sol[..., :dv]
    w = sol[..., dv:]
    p_intra = jnp.einsum('nbhid,nbhjd->nbhij', q, k) * decay
    g_last = gc[..., -1]
    qg = q * jnp.exp(gc)[..., None]
    kd = k * jnp.exp(g_last[..., None] - gc)[..., None]

    def step(state, inp):
        qg_c, kd_c, u_c, w_c, p_c, gl_c = inp
        v_new = u_c - jnp.einsum('bhcd,bhde->bhce', w_c, state)
        o = jnp.einsum('bhcd,bhde->bhce', qg_c, state) + jnp.einsum('bhij,bhje->bhie', p_c, v_new)
        state = state * jnp.exp(gl_c)[..., None, None] + jnp.einsum('bhcd,bhce->bhde', kd_c, v_new)
        return state, o

    s0 = jnp.zeros((bsz, h, dk, dv), jnp.float32)
    _, o = lax.scan(step, s0, (qg, kd, u, w, p_intra, g_last))
    return o.transpose(1, 0, 3, 2, 4).reshape(bsz, s, h, dv)


def gated_deltanet(x, w_in, conv_w, a_log, dt_bias, out_norm_g, w_out):
    bsz, s, _ = x.shape
    proj = x @ w_in
    qkv = proj[..., :2 * DN_QK_WIDTH + DN_V_WIDTH]
    z = proj[..., 2 * DN_QK_WIDTH + DN_V_WIDTH:2 * DN_QK_WIDTH + 2 * DN_V_WIDTH]
    a = proj[..., 2 * DN_QK_WIDTH + 2 * DN_V_WIDTH:2 * DN_QK_WIDTH + 2 * DN_V_WIDTH + DN_HEADS]
    b = proj[..., 2 * DN_QK_WIDTH + 2 * DN_V_WIDTH + DN_HEADS:]
    qkv = causal_short_conv(qkv, conv_w)
    q = l2norm(qkv[..., :DN_QK_WIDTH].reshape(bsz, s, DN_HEADS, DN_DK))
    k = l2norm(qkv[..., DN_QK_WIDTH:2 * DN_QK_WIDTH].reshape(bsz, s, DN_HEADS, DN_DK))
    v = qkv[..., 2 * DN_QK_WIDTH:].reshape(bsz, s, DN_HEADS, DN_DV).astype(jnp.float32)
    beta = jax.nn.sigmoid(b.astype(jnp.float32))
    g = -jnp.exp(a_log) * jax.nn.softplus(a.astype(jnp.float32) + dt_bias)
    o = chunk_gated_delta_rule(q, k, v, g, beta)
    o = o * lax.rsqrt(jnp.mean(o * o, axis=-1, keepdims=True) + NORM_EPS) * out_norm_g
    o = o * jax.nn.silu(z.reshape(bsz, s, DN_HEADS, DN_DV).astype(jnp.float32))
    return o.astype(x.dtype).reshape(bsz, s, DN_V_WIDTH) @ w_out


def shared_kv(x, w_kv):
    bsz, s, _ = x.shape
    kv = x @ w_kv
    k = kv[..., :ATT_WIDTH].reshape(bsz, s, ATT_HEADS, ATT_DH).transpose(0, 2, 1, 3)
    v = kv[..., ATT_WIDTH:].reshape(bsz, s, ATT_HEADS, ATT_DH).transpose(0, 2, 1, 3)
    return k, v


def chunked_relpos_attention(x, w_q, rel_bias_table, k_sh, v_sh, w_out):
    bsz, s, _ = x.shape
    nc = s // CHUNK
    pad = LEFT_CHUNKS * CHUNK
    q = (x @ w_q).reshape(bsz, nc, CHUNK, ATT_HEADS, ATT_DH).transpose(1, 0, 3, 2, 4) * (ATT_DH ** -0.5)
    kp = jnp.pad(k_sh, ((0, 0), (0, 0), (pad, 0), (0, 0)))
    vp = jnp.pad(v_sh, ((0, 0), (0, 0), (pad, 0), (0, 0)))
    i_loc = jnp.arange(CHUNK)
    j_band = jnp.arange(BAND)
    rel = i_loc[:, None] + pad - j_band[None, :]
    rel_idx = jnp.clip(rel, -(CHUNK - 1), MAX_REL) + (CHUNK - 1)
    bias = rel_bias_table[:, rel_idx].astype(jnp.float32)

    def one_chunk(args):
        q_c, c = args
        start = c * CHUNK
        k_band = lax.dynamic_slice_in_dim(kp, start, BAND, axis=2)
        v_band = lax.dynamic_slice_in_dim(vp, start, BAND, axis=2)
        sc = jnp.einsum('bhqd,bhkd->bhqk', q_c, k_band).astype(jnp.float32) + bias
        valid = (start - pad + j_band) >= 0
        sc = jnp.where(valid[None, None, None, :], sc, -jnp.inf)
        p = jax.nn.softmax(sc, axis=-1).astype(v_band.dtype)
        return jnp.einsum('bhqk,bhkd->bhqd', p, v_band)

    o = lax.map(one_chunk, (q, jnp.arange(nc)))
    o = o.transpose(1, 0, 3, 2, 4).reshape(bsz, s, ATT_WIDTH)
    return o @ w_out


def swiglu(x, w_gu, w_d):
    h = x @ w_gu
    f = w_d.shape[0]
    return (jax.nn.silu(h[..., :f]) * h[..., f:]) @ w_d


def routed_experts(xf, eidx, gates, w_gate_up, w_down):
    n = xf.shape[0]
    n_assign = n * TOP_K
    p_rows = n_assign + N_EXPERTS * EXPERT_BLOCK
    n_blocks = p_rows // EXPERT_BLOCK
    flat_e = eidx.reshape(-1)
    order = jnp.argsort(flat_e)
    sorted_e = flat_e[order]
    sorted_tok = order // TOP_K
    sorted_w = gates.reshape(-1)[order]
    counts = jnp.zeros((N_EXPERTS,), jnp.int32).at[flat_e].add(1)
    padded = (counts + EXPERT_BLOCK - 1) // EXPERT_BLOCK * EXPERT_BLOCK
    start = jnp.cumsum(counts) - counts
    pend = jnp.cumsum(padded)
    pstart = pend - padded
    dest = pstart[sorted_e] + jnp.arange(n_assign) - start[sorted_e]
    buf_tok = jnp.zeros((p_rows,), jnp.int32).at[dest].set(sorted_tok.astype(jnp.int32))
    buf_w = jnp.zeros((p_rows,), gates.dtype).at[dest].set(sorted_w)
    block_e = jnp.minimum(jnp.searchsorted(pend, jnp.arange(n_blocks) * EXPERT_BLOCK, side='right'),
                          N_EXPERTS - 1)

    def step(out, inp):
        tok, wt, e = inp
        y = swiglu(xf[tok], w_gate_up[e], w_down[e])
        return out.at[tok].add(y * wt[:, None]), None

    out, _ = lax.scan(step, jnp.zeros_like(xf),
                      (buf_tok.reshape(n_blocks, EXPERT_BLOCK), buf_w.reshape(n_blocks, EXPERT_BLOCK), block_e))
    return out


def moe_ffn(x, w_router, router_bias, w_gate_up, w_down, w_shared_gu, w_shared_down):
    bsz, s, d = x.shape
    xf = x.reshape(-1, d)
    n = xf.shape[0]
    scores = jax.nn.sigmoid((xf @ w_router).astype(jnp.float32))
    sel = scores + router_bias.astype(jnp.float32)
    grp_score = jnp.sum(lax.top_k(sel.reshape(n, N_GROUPS, N_EXPERTS // N_GROUPS), 2)[0], axis=-1)
    _, top_groups = lax.top_k(grp_score, TOPK_GROUPS)
    gmask = jnp.sum(jax.nn.one_hot(top_groups, N_GROUPS, dtype=jnp.float32), axis=-2) > 0
    sel = jnp.where(jnp.repeat(gmask, N_EXPERTS // N_GROUPS, axis=-1), sel, -jnp.inf)
    _, eidx = lax.top_k(sel, TOP_K)
    wts = jnp.take_along_axis(scores, eidx, axis=-1)
    wts = wts / jnp.sum(wts, axis=-1, keepdims=True) * ROUTED_SCALE
    routed = routed_experts(xf, eidx, wts.astype(x.dtype), w_gate_up, w_down)
    shared = swiglu(xf, w_shared_gu, w_shared_down)
    return (routed + shared).reshape(bsz, s, d)


def setup_inputs(seed: int = 0) -> dict:
    key = jax.random.key(seed)
    ks = jax.random.split(key, 24)
    f32 = jnp.float32
    nrm = lambda k, shape, scale: jax.random.normal(k, shape, f32) * scale
    x = jax.random.normal(ks[0], (BATCH, SEQ, D_MODEL), f32)
    a_col_scale = jnp.ones((DN_IN_WIDTH,), f32).at[2 * DN_QK_WIDTH:2 * DN_QK_WIDTH + DN_V_WIDTH].set(DN_BETA)
    a_w_in = nrm(ks[1], (N_A_LAYERS, D_MODEL, DN_IN_WIDTH), D_MODEL ** -0.5) * a_col_scale
    a_conv_w = nrm(ks[2], (N_A_LAYERS, CONV_WIDTH, 2 * DN_QK_WIDTH + DN_V_WIDTH), CONV_WIDTH ** -0.5)
    a_a_log = jnp.log(jax.random.uniform(ks[3], (N_A_LAYERS, DN_HEADS), f32, 1.0, 16.0))
    dt = jnp.exp(jax.random.uniform(ks[4], (N_A_LAYERS, DN_HEADS), f32, math.log(1e-3), math.log(1e-1)))
    a_dt_bias = dt + jnp.log(-jnp.expm1(-dt))
    a_out_norm_g = 1.0 + nrm(ks[5], (N_A_LAYERS, DN_DV), 0.02)
    a_w_out = nrm(ks[6], (N_A_LAYERS, DN_V_WIDTH, D_MODEL), DN_V_WIDTH ** -0.5 * DN_BETA)
    kv_col_scale = jnp.ones((2 * ATT_WIDTH,), f32).at[ATT_WIDTH:].set(DN_BETA)
    w_kv_shared = nrm(ks[7], (D_MODEL, 2 * ATT_WIDTH), D_MODEL ** -0.5) * kv_col_scale
    b_w_q = nrm(ks[8], (N_B_LAYERS, D_MODEL, ATT_WIDTH), D_MODEL ** -0.5)
    b_rel_bias = nrm(ks[9], (N_B_LAYERS, ATT_HEADS, REL_BIAS_SIZE), 0.2)
    b_w_out = nrm(ks[10], (N_B_LAYERS, ATT_WIDTH, D_MODEL), ATT_WIDTH ** -0.5 * DN_BETA)
    moe_w_router = nrm(ks[11], (DEPTH, D_MODEL, N_EXPERTS), D_MODEL ** -0.5)
    moe_router_bias = nrm(ks[12], (DEPTH, N_EXPERTS), 0.01)
    moe_w_gate_up = nrm(ks[13], (DEPTH, N_EXPERTS, D_MODEL, 2 * D_EXPERT), D_MODEL ** -0.5)
    moe_w_down = nrm(ks[14], (DEPTH, N_EXPERTS, D_EXPERT, D_MODEL), D_EXPERT ** -0.5 * DN_BETA)
    moe_w_shared_gate_up = nrm(ks[15], (DEPTH, D_MODEL, 2 * D_SHARED), D_MODEL ** -0.5)
    moe_w_shared_down = nrm(ks[16], (DEPTH, D_SHARED, D_MODEL), D_SHARED ** -0.5 * DN_BETA)
    ln_mix_g = 1.0 + nrm(ks[17], (DEPTH, D_MODEL), 0.02)
    ln_mix_b = nrm(ks[18], (DEPTH, D_MODEL), 0.02)
    ln_ffn_g = 1.0 + nrm(ks[19], (DEPTH, D_MODEL), 0.02)
    ln_ffn_b = nrm(ks[20], (DEPTH, D_MODEL), 0.02)
    return {"x": x, "a_w_in": a_w_in, "a_conv_w": a_conv_w, "a_a_log": a_a_log,
            "a_dt_bias": a_dt_bias, "a_out_norm_g": a_out_norm_g, "a_w_out": a_w_out,
            "w_kv_shared": w_kv_shared, "b_w_q": b_w_q, "b_rel_bias": b_rel_bias,
            "b_w_out": b_w_out, "moe_w_router": moe_w_router, "moe_router_bias": moe_router_bias,
            "moe_w_gate_up": moe_w_gate_up, "moe_w_down": moe_w_down,
            "moe_w_shared_gate_up": moe_w_shared_gate_up, "moe_w_shared_down": moe_w_shared_down,
            "ln_mix_g": ln_mix_g, "ln_mix_b": ln_mix_b, "ln_ffn_g": ln_ffn_g, "ln_ffn_b": ln_ffn_b}


def reference(x, a_w_in, a_conv_w, a_a_log, a_dt_bias, a_out_norm_g, a_w_out,
              w_kv_shared, b_w_q, b_rel_bias, b_w_out, moe_w_router, moe_router_bias,
              moe_w_gate_up, moe_w_down, moe_w_shared_gate_up, moe_w_shared_down,
              ln_mix_g, ln_mix_b, ln_ffn_g, ln_ffn_b):
    k_sh = None
    v_sh = None
    for layer in range(DEPTH):
        if layer < N_A_LAYERS:
            mix = gated_deltanet(x, a_w_in[layer], a_conv_w[layer], a_a_log[layer],
                                 a_dt_bias[layer], a_out_norm_g[layer], a_w_out[layer])
        else:
            if layer == N_A_LAYERS:
                k_sh, v_sh = shared_kv(x, w_kv_shared)
            j = layer - N_A_LAYERS
            mix = chunked_relpos_attention(x, b_w_q[j], b_rel_bias[j], k_sh, v_sh, b_w_out[j])
        x = layer_norm(DN_ALPHA * x + mix, ln_mix_g[layer], ln_mix_b[layer])
        ffn = moe_ffn(x, moe_w_router[layer], moe_router_bias[layer], moe_w_gate_up[layer],
                      moe_w_down[layer], moe_w_shared_gate_up[layer], moe_w_shared_down[layer])
        x = layer_norm(DN_ALPHA * x + ffn, ln_ffn_g[layer], ln_ffn_b[layer])
    return x
```

```python
import functools

import jax
import jax.numpy as jnp
from jax import lax
from jax.experimental import pallas as pl
from jax.experimental.pallas import tpu as pltpu

F32 = jnp.float32
BF16 = jnp.bfloat16
HIGHEST = lax.Precision.HIGHEST

CHUNK = 64
DN_HEADS = 8
DN_DK = 128
CONV_WIDTH = 4
ATT_HEADS = 16
ATT_DH = 64
LEFT_CHUNKS = 8
MAX_REL = 256
N_EXPERTS = 256
TOP_K = 8
N_GROUPS = 8
TOPK_GROUPS = 4
ROUTED_SCALE = 2.5
DEPTH = 2
DN_ALPHA = (2 * DEPTH) ** 0.25
LN_EPS = 1e-5
NORM_EPS = 1e-6

LANES = 128
VMEM_LIMIT = 56 * 1024 * 1024

DN_TILE = 2 * CHUNK
ATT_TILE = 8 * CHUNK
ATT_BAND = (LEFT_CHUNKS + 2) * CHUNK
MOE_BLK = 256
COPY_BLK = 256
ROUTER_TILE = 512


def _cparams(sem):
    return pltpu.CompilerParams(dimension_semantics=sem, vmem_limit_bytes=VMEM_LIMIT)


def _sigmoid(x):
    return 1.0 / (1.0 + jnp.exp(-x))


def _dot(a, b):
    return jnp.dot(a, b, preferred_element_type=F32)


def _dot_nt(a, b, precision=None):
    return lax.dot_general(a, b, (((1,), (1,)), ((), ())), precision=precision,
                           preferred_element_type=F32)


def _dot_tn(a, b):
    return lax.dot_general(a, b, (((0,), (0,)), ((), ())), preferred_element_type=F32)


def _layer_norm(y, g, b):
    mu = jnp.mean(y, axis=-1, keepdims=True)
    d = y - mu
    var = jnp.mean(d * d, axis=-1, keepdims=True)
    return d * lax.rsqrt(var + LN_EPS) * g + b


def _mm_kernel(a_ref, w_ref, o_ref):
    o_ref[...] = _dot(a_ref[...], w_ref[...]).astype(o_ref.dtype)


def _matmul(a, w, out_dtype, tm, tn):
    m, k = a.shape
    n = w.shape[1]
    return pl.pallas_call(
        _mm_kernel,
        out_shape=jax.ShapeDtypeStruct((m, n), out_dtype),
        grid=(m // tm, n // tn),
        in_specs=[pl.BlockSpec((tm, k), lambda i, j: (i, 0)),
                  pl.BlockSpec((k, tn), lambda i, j: (0, j))],
        out_specs=pl.BlockSpec((tm, tn), lambda i, j: (i, j)),
        compiler_params=_cparams(("parallel", "arbitrary")),
        name="matmul",
    )(a, w)


def _mm_res_ln_kernel(a_ref, w_ref, x_ref, g_ref, b_ref, o_ref, ob_ref):
    y = DN_ALPHA * x_ref[...] + _dot(a_ref[...], w_ref[...])
    out = _layer_norm(y, g_ref[...], b_ref[...])
    o_ref[...] = out
    ob_ref[...] = out.astype(BF16)


def _matmul_res_ln(a, w, x, g, b, tm=512):
    m, k = a.shape
    d = w.shape[1]
    return pl.pallas_call(
        _mm_res_ln_kernel,
        out_shape=(jax.ShapeDtypeStruct((m, d), F32), jax.ShapeDtypeStruct((m, d), BF16)),
        grid=(m // tm,),
        in_specs=[pl.BlockSpec((tm, k), lambda i: (i, 0)),
                  pl.BlockSpec((k, d), lambda i: (0, 0)),
                  pl.BlockSpec((tm, d), lambda i: (i, 0)),
                  pl.BlockSpec((1, d), lambda i: (0, 0)),
                  pl.BlockSpec((1, d), lambda i: (0, 0))],
        out_specs=(pl.BlockSpec((tm, d), lambda i: (i, 0)),
                   pl.BlockSpec((tm, d), lambda i: (i, 0))),
        compiler_params=_cparams(("parallel",)),
        name="matmul_res_ln",
    )(a, w, x, g.reshape(1, d), b.reshape(1, d))


def _dn_kernel(qkv_ref, z_ref, ab_ref, cw_ref, alog_ref, dtb_ref, ong_ref, o_ref,
               xc_ref, s_ref):
    t = DN_TILE
    dk = DN_DK
    qk_w = DN_HEADS * dk
    c_idx = pl.program_id(1)

    @pl.when(c_idx == 0)
    def _():
        xc_ref[0:8, :] = jnp.zeros((8, xc_ref.shape[1]), F32)
        s_ref[...] = jnp.zeros(s_ref.shape, F32)

    xc_ref[8:8 + t, :] = qkv_ref[...]

    ab = ab_ref[...]
    a_sh = ab + dtb_ref[...]
    softplus = jnp.maximum(a_sh, 0.0) + jnp.log(1.0 + jnp.exp(-jnp.abs(a_sh)))
    g_full = -jnp.exp(alog_ref[...]) * softplus
    beta_full = _sigmoid(ab)

    row = lax.broadcasted_iota(jnp.int32, (t, t), 0)
    col = lax.broadcasted_iota(jnp.int32, (t, t), 1)
    same = (row // CHUNK) == (col // CHUNK)
    m_incl = same & (row >= col)
    m_strict = same & (row > col)
    l_incl = jnp.where(m_incl, 1.0, 0.0).astype(F32)
    l_all = jnp.where(same, 1.0, 0.0).astype(F32)
    gc_all = jnp.dot(l_incl, g_full, precision=HIGHEST, preferred_element_type=F32)
    gl_all = jnp.dot(l_all, g_full, precision=HIGHEST, preferred_element_type=F32)
    gc_t = gc_all.T

    def conv_act(off):
        acc = xc_ref[5:5 + t, off:off + dk] * cw_ref[0:1, off:off + dk]
        for j in range(1, CONV_WIDTH):
            acc = acc + xc_ref[5 + j:5 + j + t, off:off + dk] * cw_ref[j:j + 1, off:off + dk]
        return acc * _sigmoid(acc)

    for h in range(DN_HEADS):
        q = conv_act(h * dk)
        k = conv_act(qk_w + h * dk)
        v = conv_act(2 * qk_w + h * dk)
        q = q * lax.rsqrt(jnp.sum(q * q, axis=-1, keepdims=True) + NORM_EPS) * (dk ** -0.5)
        k = k * lax.rsqrt(jnp.sum(k * k, axis=-1, keepdims=True) + NORM_EPS)
        beta = beta_full[:, DN_HEADS + h:DN_HEADS + h + 1]
        gc = gc_all[:, h:h + 1]
        gl = gl_all[:, h:h + 1]
        gr = gc_t[h:h + 1, :]
        decay = jnp.exp(jnp.where(m_incl, gc - gr, -jnp.inf))
        kb = k * beta
        k_b = k.astype(BF16)
        a_mat = jnp.where(m_strict, _dot_nt(kb.astype(BF16), k_b) * decay, 0.0)
        p_mat = _dot_nt(q.astype(BF16), k_b) * decay
        egc = jnp.exp(gc)
        y = jnp.concatenate([v * beta, kb * egc], axis=1)
        a_b = a_mat.astype(BF16)
        y = y - _dot(a_b, y.astype(BF16))
        for _ in range(5):
            a_b = _dot(a_b, a_b).astype(BF16)
            y = y + _dot(a_b, y.astype(BF16))
        u = y[:, :dk]
        w = y[:, dk:]
        qg = q * egc
        kd = k * jnp.exp(gl - gc)

        s0 = s_ref[h]
        r0 = _dot(jnp.concatenate([w[:CHUNK], qg[:CHUNK]], axis=0).astype(BF16), s0.astype(BF16))
        vn0 = u[:CHUNK] - r0[:CHUNK]
        s1 = s0 * jnp.exp(gl[0:1, :]) + _dot_tn(kd[:CHUNK].astype(BF16), vn0.astype(BF16))
        r1 = _dot(jnp.concatenate([w[CHUNK:], qg[CHUNK:]], axis=0).astype(BF16), s1.astype(BF16))
        vn1 = u[CHUNK:] - r1[:CHUNK]
        s2 = s1 * jnp.exp(gl[CHUNK:CHUNK + 1, :]) + _dot_tn(kd[CHUNK:].astype(BF16), vn1.astype(BF16))
        s_ref[h] = s2

        vn = jnp.concatenate([vn0, vn1], axis=0)
        o = jnp.concatenate([r0[CHUNK:], r1[CHUNK:]], axis=0) + _dot(p_mat.astype(BF16), vn.astype(BF16))
        o = o * lax.rsqrt(jnp.mean(o * o, axis=-1, keepdims=True) + NORM_EPS) * ong_ref[...]
        zh = z_ref[:, h * dk:(h + 1) * dk]
        o = o * (zh * _sigmoid(zh))
        o_ref[:, h * dk:(h + 1) * dk] = o.astype(o_ref.dtype)

    xc_ref[0:8, :] = xc_ref[t:t + 8, :]


def _deltanet(proj, ab, conv_w, a_log, dt_bias, out_norm_g, bsz, seq):
    n = proj.shape[0]
    t = DN_TILE
    nct = seq // t
    qkv_w = 3 * DN_HEADS * DN_DK
    v_w = DN_HEADS * DN_DK
    alog_p = jnp.zeros((1, LANES), F32).at[0, :DN_HEADS].set(a_log)
    dtb_p = jnp.zeros((1, LANES), F32).at[0, :DN_HEADS].set(dt_bias)
    return pl.pallas_call(
        _dn_kernel,
        out_shape=jax.ShapeDtypeStruct((n, v_w), BF16),
        grid=(bsz, nct),
        in_specs=[pl.BlockSpec((t, qkv_w), lambda b, c: (b * nct + c, 0)),
                  pl.BlockSpec((t, v_w), lambda b, c: (b * nct + c, qkv_w // v_w)),
                  pl.BlockSpec((t, LANES), lambda b, c: (b * nct + c, 0)),
                  pl.BlockSpec((CONV_WIDTH, qkv_w), lambda b, c: (0, 0)),
                  pl.BlockSpec((1, LANES), lambda b, c: (0, 0)),
                  pl.BlockSpec((1, LANES), lambda b, c: (0, 0)),
                  pl.BlockSpec((1, DN_DK), lambda b, c: (0, 0))],
        out_specs=pl.BlockSpec((t, v_w), lambda b, c: (b * nct + c, 0)),
        scratch_shapes=[pltpu.VMEM((t + 8, qkv_w), F32),
                        pltpu.VMEM((DN_HEADS, DN_DK, DN_DK), F32)],
        compiler_params=_cparams(("arbitrary", "arbitrary")),
        name="deltanet",
    )(proj, proj, ab, conv_w, alog_p, dtb_p, out_norm_g.reshape(1, DN_DK))


def _att_kernel(q_ref, kp_ref, kc_ref, vp_ref, vc_ref, bias_ref, o_ref, ks_ref, vs_ref):
    tq = ATT_TILE
    t_idx = pl.program_id(1)
    d = q_ref.shape[1]
    zeros = jnp.zeros((CHUNK, d), BF16)
    ks_ref[0:CHUNK, :] = zeros
    vs_ref[0:CHUNK, :] = zeros
    ks_ref[CHUNK:CHUNK + tq, :] = kp_ref[...]
    vs_ref[CHUNK:CHUNK + tq, :] = vp_ref[...]
    ks_ref[CHUNK + tq:CHUNK + 2 * tq, :] = kc_ref[...]
    vs_ref[CHUNK + tq:CHUNK + 2 * tq, :] = vc_ref[...]

    lane = lax.broadcasted_iota(jnp.int32, (CHUNK, 2 * ATT_DH), 1)
    first = lane < ATT_DH
    j_band = lax.broadcasted_iota(jnp.int32, (2 * CHUNK, ATT_BAND), 1)

    def chunk_body(c, carry):
        r0 = pl.multiple_of(c * CHUNK, CHUNK)
        kpos = t_idx * tq - (LEFT_CHUNKS + 1) * CHUNK + c * CHUNK + j_band
        valid = (j_band >= CHUNK) & (kpos >= 0)
        for hp in range(ATT_HEADS // 2):
            ls = slice(hp * 2 * ATT_DH, (hp + 1) * 2 * ATT_DH)
            qp = q_ref[pl.ds(r0, CHUNK), ls]
            zq = jnp.zeros_like(qp)
            q2 = jnp.concatenate([jnp.where(first, qp, zq), jnp.where(first, zq, qp)], axis=0)
            kb = ks_ref[pl.ds(r0, ATT_BAND), ls]
            vb = vs_ref[pl.ds(r0, ATT_BAND), ls]
            s = _dot_nt(q2, kb) * (ATT_DH ** -0.5)
            s = s + jnp.concatenate([bias_ref[2 * hp], bias_ref[2 * hp + 1]], axis=0)
            s = jnp.where(valid, s, -jnp.inf)
            m = jnp.max(s, axis=-1, keepdims=True)
            p = jnp.exp(s - m)
            l = jnp.sum(p, axis=-1, keepdims=True)
            p = p / l
            r = _dot(p.astype(BF16), vb)
            o_ref[pl.ds(r0, CHUNK), ls] = jnp.where(first, r[:CHUNK], r[CHUNK:]).astype(o_ref.dtype)
        return carry

    lax.fori_loop(0, tq // CHUNK, chunk_body, 0)


def _attention(q, kv, bias, bsz, seq):
    n, d = q.shape
    tq = ATT_TILE
    nt = seq // tq
    rows = CHUNK + 2 * tq
    prev = lambda b, t: (b * nt + jnp.maximum(t - 1, 0), 0)
    cur = lambda b, t: (b * nt + t, 0)
    prev_v = lambda b, t: (b * nt + jnp.maximum(t - 1, 0), 1)
    cur_v = lambda b, t: (b * nt + t, 1)
    return pl.pallas_call(
        _att_kernel,
        out_shape=jax.ShapeDtypeStruct((n, d), BF16),
        grid=(bsz, nt),
        in_specs=[pl.BlockSpec((tq, d), cur),
                  pl.BlockSpec((tq, d), prev),
                  pl.BlockSpec((tq, d), cur),
                  pl.BlockSpec((tq, d), prev_v),
                  pl.BlockSpec((tq, d), cur_v),
                  pl.BlockSpec((ATT_HEADS, CHUNK, ATT_BAND), lambda b, t: (0, 0, 0))],
        out_specs=pl.BlockSpec((tq, d), cur),
        scratch_shapes=[pltpu.VMEM((rows, d), BF16), pltpu.VMEM((rows, d), BF16)],
        compiler_params=_cparams(("parallel", "arbitrary")),
        name="band_attention",
    )(q, kv, kv, kv, kv, bias)


def _rel_bias(table):
    i_loc = jnp.arange(CHUNK)
    j_band = jnp.arange(ATT_BAND)
    rel = i_loc[:, None] + (LEFT_CHUNKS + 1) * CHUNK - j_band[None, :]
    rel_idx = jnp.clip(rel, -(CHUNK - 1), MAX_REL) + (CHUNK - 1)
    return table[:, rel_idx].astype(F32)


def _router_kernel(x_ref, wt_ref, bias_ref, idx_ref, gate_ref):
    tt = x_ref.shape[0]
    ne = wt_ref.shape[0]
    gsz = ne // N_GROUPS
    logits = _dot_nt(wt_ref[...], x_ref[...], precision=HIGHEST)
    scores = _sigmoid(logits)
    sel = scores + bias_ref[...]
    neg = -jnp.inf

    iota_g = lax.broadcasted_iota(jnp.int32, (gsz, tt), 0)
    gs_rows = []
    for g in range(N_GROUPS):
        blk = sel[g * gsz:(g + 1) * gsz, :]
        m1 = jnp.max(blk, axis=0, keepdims=True)
        i1 = jnp.min(jnp.where(blk == m1, iota_g, gsz), axis=0, keepdims=True)
        m2 = jnp.max(jnp.where(iota_g == i1, neg, blk), axis=0, keepdims=True)
        gs_rows.append(m1 + m2)
    gsc = jnp.concatenate(gs_rows, axis=0)
    iota_n = lax.broadcasted_iota(jnp.int32, (N_GROUPS, tt), 0)
    chosen = jnp.zeros((N_GROUPS, tt), F32)
    for _ in range(TOPK_GROUPS):
        m = jnp.max(gsc, axis=0, keepdims=True)
        i = jnp.min(jnp.where(gsc == m, iota_n, N_GROUPS), axis=0, keepdims=True)
        hit = iota_n == i
        chosen = jnp.where(hit, 1.0, chosen)
        gsc = jnp.where(hit, neg, gsc)
    selm = jnp.concatenate(
        [jnp.where(chosen[g:g + 1, :] > 0.0, sel[g * gsz:(g + 1) * gsz, :], neg)
         for g in range(N_GROUPS)], axis=0)

    iota_e = lax.broadcasted_iota(jnp.int32, (ne, tt), 0)
    idx_rows, w_rows = [], []
    for _ in range(TOP_K):
        m = jnp.max(selm, axis=0, keepdims=True)
        i = jnp.min(jnp.where(selm == m, iota_e, ne), axis=0, keepdims=True)
        hit = iota_e == i
        w_rows.append(jnp.sum(jnp.where(hit, scores, 0.0), axis=0, keepdims=True))
        idx_rows.append(i)
        selm = jnp.where(hit, neg, selm)
    wts = jnp.concatenate(w_rows, axis=0)
    wts = wts / jnp.sum(wts, axis=0, keepdims=True) * ROUTED_SCALE
    idx_ref[...] = jnp.concatenate(idx_rows, axis=0)
    gate_ref[...] = wts


def _router(x, w_router, router_bias):
    n, d = x.shape
    ne = w_router.shape[1]
    tt = ROUTER_TILE
    return pl.pallas_call(
        _router_kernel,
        out_shape=(jax.ShapeDtypeStruct((TOP_K, n), jnp.int32),
                   jax.ShapeDtypeStruct((TOP_K, n), F32)),
        grid=(n // tt,),
        in_specs=[pl.BlockSpec((tt, d), lambda i: (i, 0)),
                  pl.BlockSpec((ne, d), lambda i: (0, 0)),
                  pl.BlockSpec((ne, 1), lambda i: (0, 0))],
        out_specs=(pl.BlockSpec((TOP_K, tt), lambda i: (0, i)),
                   pl.BlockSpec((TOP_K, tt), lambda i: (0, i))),
        compiler_params=_cparams(("parallel",)),
        name="router",
    )(x, w_router.T, router_bias.reshape(ne, 1))


def _row_copy_kernel(nv_ref, sidx_ref, didx_ref, src_ref, out_ref, sem):
    i = pl.program_id(0)
    nrows = sidx_ref.shape[2]

    def row_copy(r):
        return pltpu.make_async_copy(src_ref.at[pl.ds(sidx_ref[0, 0, r], 1)],
                                     out_ref.at[pl.ds(didx_ref[0, 0, r], 1)], sem)

    @pl.when(i < nv_ref[0])
    def _():
        for r in range(nrows):
            row_copy(r).start()
        for r in range(nrows):
            row_copy(r).wait()


def _row_copy(src, src_idx, dst_idx, n_valid_blocks, out_rows):
    nblk = src_idx.shape[0] // COPY_BLK
    sidx = src_idx.reshape(nblk, 1, COPY_BLK)
    didx = dst_idx.reshape(nblk, 1, COPY_BLK)
    grid_spec = pltpu.PrefetchScalarGridSpec(
        num_scalar_prefetch=1,
        grid=(nblk,),
        in_specs=[pl.BlockSpec((1, 1, COPY_BLK), lambda i, nv: (i, 0, 0), memory_space=pltpu.SMEM),
                  pl.BlockSpec((1, 1, COPY_BLK), lambda i, nv: (i, 0, 0), memory_space=pltpu.SMEM),
                  pl.BlockSpec(memory_space=pl.ANY)],
        out_specs=pl.BlockSpec(memory_space=pl.ANY),
        scratch_shapes=[pltpu.SemaphoreType.DMA(())],
    )
    return pl.pallas_call(
        _row_copy_kernel,
        out_shape=jax.ShapeDtypeStruct((out_rows, src.shape[1]), src.dtype),
        grid_spec=grid_spec,
        compiler_params=pltpu.CompilerParams(dimension_semantics=("arbitrary",)),
        name="row_copy",
    )(n_valid_blocks, sidx, didx, src)


def _expert_kernel(be_ref, nv_ref, x_ref, wgu_ref, wd_ref, y_ref):
    i = pl.program_id(0)
    f = wd_ref.shape[1]

    @pl.when(i < nv_ref[0])
    def _():
        h = _dot(x_ref[...].astype(BF16), wgu_ref[0].astype(BF16))
        g = h[:, :f]
        act = g * _sigmoid(g) * h[:, f:]
        y_ref[...] = _dot(act.astype(BF16), wd_ref[0].astype(BF16))

    @pl.when(i >= nv_ref[0])
    def _():
        y_ref[...] = jnp.zeros(y_ref.shape, y_ref.dtype)


def _experts(xs, block_e, n_valid_blocks, w_gate_up, w_down):
    p, d = xs.shape
    nblk = p // MOE_BLK
    f2 = w_gate_up.shape[2]
    f = w_down.shape[1]
    grid_spec = pltpu.PrefetchScalarGridSpec(
        num_scalar_prefetch=2,
        grid=(nblk,),
        in_specs=[pl.BlockSpec((MOE_BLK, d), lambda i, be, nv: (jnp.minimum(i, nv[0] - 1), 0)),
                  pl.BlockSpec((1, d, f2), lambda i, be, nv: (be[i], 0, 0)),
                  pl.BlockSpec((1, f, d), lambda i, be, nv: (be[i], 0, 0))],
        out_specs=pl.BlockSpec((MOE_BLK, d), lambda i, be, nv: (i, 0)),
    )
    return pl.pallas_call(
        _expert_kernel,
        out_shape=jax.ShapeDtypeStruct((p, d), F32),
        grid_spec=grid_spec,
        compiler_params=_cparams(("arbitrary",)),
        name="routed_experts",
    )(block_e, n_valid_blocks, xs, w_gate_up, w_down)


def _moe_final_kernel(x_ref, *refs):
    y_refs = refs[:TOP_K]
    gate_ref, wsg_ref, wsd_ref, g_ref, b_ref, o_ref, ob_ref = refs[TOP_K:]
    x = x_ref[...]
    f = wsd_ref.shape[0]
    h = _dot(x.astype(BF16), wsg_ref[...])
    g = h[:, :f]
    acc = _dot((g * _sigmoid(g) * h[:, f:]).astype(BF16), wsd_ref[...])
    gates = gate_ref[...]
    for k in range(TOP_K):
        acc = acc + y_refs[k][...] * gates[:, k:k + 1]
    out = _layer_norm(DN_ALPHA * x + acc, g_ref[...], b_ref[...])
    o_ref[...] = out
    ob_ref[...] = out.astype(BF16)


def _moe_final(x, y_tok, gates, w_sgu, w_sd, g, b, tm=256):
    n, d = x.shape
    nb = n // tm
    y_specs = [pl.BlockSpec((tm, d), functools.partial(lambda i, k: (k * nb + i, 0), k=k))
               for k in range(TOP_K)]
    return pl.pallas_call(
        _moe_final_kernel,
        out_shape=(jax.ShapeDtypeStruct((n, d), F32), jax.ShapeDtypeStruct((n, d), BF16)),
        grid=(nb,),
        in_specs=[pl.BlockSpec((tm, d), lambda i: (i, 0))] + y_specs + [
            pl.BlockSpec((tm, TOP_K), lambda i: (i, 0)),
            pl.BlockSpec(w_sgu.shape, lambda i: (0, 0)),
            pl.BlockSpec(w_sd.shape, lambda i: (0, 0)),
            pl.BlockSpec((1, d), lambda i: (0, 0)),
            pl.BlockSpec((1, d), lambda i: (0, 0))],
        out_specs=(pl.BlockSpec((tm, d), lambda i: (i, 0)),
                   pl.BlockSpec((tm, d), lambda i: (i, 0))),
        compiler_params=_cparams(("parallel",)),
        name="moe_final",
    )(x, *([y_tok] * TOP_K), gates, w_sgu, w_sd, g.reshape(1, d), b.reshape(1, d))


def _dispatch_plan(eidx_t):
    k, n = eidx_t.shape
    n_assign = k * n
    p_rows = n_assign + N_EXPERTS * MOE_BLK
    nblk = p_rows // MOE_BLK
    flat_e = eidx_t.reshape(-1)
    order = jnp.argsort(flat_e, stable=True).astype(jnp.int32)
    sorted_e = flat_e[order]
    counts = jnp.zeros((N_EXPERTS,), jnp.int32).at[flat_e].add(1)
    padded = (counts + MOE_BLK - 1) // MOE_BLK * MOE_BLK
    start = jnp.cumsum(counts) - counts
    pend = jnp.cumsum(padded)
    pstart = pend - padded
    dest = pstart[sorted_e] + jnp.arange(n_assign, dtype=jnp.int32) - start[sorted_e]
    pos = jnp.arange(p_rows, dtype=jnp.int32)
    dummy = n_assign + ((pos // MOE_BLK) % 2) * MOE_BLK + pos % MOE_BLK
    buf_tok = jnp.zeros((p_rows,), jnp.int32).at[dest].set(order % n)
    buf_dst = dummy.at[dest].set(order)
    n_valid = (pend[-1] // MOE_BLK).astype(jnp.int32)
    blk_start = jnp.arange(nblk, dtype=jnp.int32) * MOE_BLK
    block_e = jnp.minimum(jnp.searchsorted(pend, blk_start, side='right'), N_EXPERTS - 1).astype(jnp.int32)
    last_e = block_e[jnp.maximum(n_valid - 1, 0)]
    block_e = jnp.where(jnp.arange(nblk) < n_valid, block_e, last_e)
    return buf_tok, buf_dst, pos, block_e, n_valid.reshape(1), n_assign + 2 * MOE_BLK


def _moe_layer(x_f32, w_router, router_bias, w_gate_up, w_down, w_sgu, w_sd, ln_g, ln_b):
    eidx_t, gates_t = _router(x_f32, w_router, router_bias)
    buf_tok, buf_dst, pos, block_e, n_valid, ytok_rows = _dispatch_plan(eidx_t)
    xs = _row_copy(x_f32, buf_tok, pos, n_valid, buf_tok.shape[0])
    ys = _experts(xs, block_e, n_valid, w_gate_up, w_down)
    y_tok = _row_copy(ys, pos, buf_dst, n_valid, ytok_rows)
    return _moe_final(x_f32, y_tok, gates_t.T, w_sgu.astype(BF16), w_sd.astype(BF16), ln_g, ln_b)


def kernel(x, a_w_in, a_conv_w, a_a_log, a_dt_bias, a_out_norm_g, a_w_out, w_kv_shared, b_w_q, b_rel_bias, b_w_out, moe_w_router, moe_router_bias, moe_w_gate_up, moe_w_down, moe_w_shared_gate_up, moe_w_shared_down, ln_mix_g, ln_mix_b, ln_ffn_g, ln_ffn_b):
    bsz, seq, d = x.shape
    n = bsz * seq
    xf = x.reshape(n, d)
    xb = xf.astype(BF16)

    main_w = 4 * DN_HEADS * DN_DK
    w_in = a_w_in[0]
    proj = _matmul(xb, w_in[:, :main_w].astype(BF16), F32, 512, 1024)
    w_ab = jnp.zeros((d, LANES), F32).at[:, :2 * DN_HEADS].set(w_in[:, main_w:]).astype(BF16)
    ab = _matmul(xb, w_ab, F32, 512, LANES)
    o = _deltanet(proj, ab, a_conv_w[0], a_a_log[0], a_dt_bias[0], a_out_norm_g[0], bsz, seq)
    x1, _ = _matmul_res_ln(o, a_w_out[0].astype(BF16), xf, ln_mix_g[0], ln_mix_b[0])
    x2, x2b = _moe_layer(x1, moe_w_router[0], moe_router_bias[0], moe_w_gate_up[0], moe_w_down[0],
                         moe_w_shared_gate_up[0], moe_w_shared_down[0], ln_ffn_g[0], ln_ffn_b[0])

    kv = _matmul(x2b, w_kv_shared.astype(BF16), BF16, 512, 1024)
    q = _matmul(x2b, b_w_q[0].astype(BF16), BF16, 512, 1024)
    att = _attention(q, kv, _rel_bias(b_rel_bias[0]), bsz, seq)
    x3, _ = _matmul_res_ln(att, b_w_out[0].astype(BF16), x2, ln_mix_g[1], ln_mix_b[1])
    x4, _ = _moe_layer(x3, moe_w_router[1], moe_router_bias[1], moe_w_gate_up[1], moe_w_down[1],
                       moe_w_shared_gate_up[1], moe_w_shared_down[1], ln_ffn_g[1], ln_ffn_b[1])
    return x4.reshape(bsz, seq, d)
```

```python
import jax
import jax.numpy as jnp
from jax import lax
from jax.experimental import pallas as pl
from jax.experimental.pallas import tpu as pltpu

F32 = jnp.float32
BF16 = jnp.bfloat16
HIGHEST = lax.Precision.HIGHEST

CHUNK = 64
DN_HEADS = 8
DN_DK = 128
CONV_WIDTH = 4
ATT_HEADS = 16
ATT_DH = 64
LEFT_CHUNKS = 8
MAX_REL = 256
N_EXPERTS = 256
TOP_K = 8
N_GROUPS = 8
TOPK_GROUPS = 4
ROUTED_SCALE = 2.5
DEPTH = 2
DN_ALPHA = (2 * DEPTH) ** 0.25
LN_EPS = 1e-5
NORM_EPS = 1e-6

LANES = 128
VMEM_LIMIT = 56 * 1024 * 1024

DN_TILE = 2 * CHUNK
ATT_TILE = 8 * CHUNK
ATT_BAND = (LEFT_CHUNKS + 2) * CHUNK
MOE_BLK = 256
DISPATCH_TILE = 256
ROUTER_TILE = 512


def _cparams(sem):
    return pltpu.CompilerParams(dimension_semantics=sem, vmem_limit_bytes=VMEM_LIMIT)


def _sigmoid(x):
    return 1.0 / (1.0 + jnp.exp(-x))


def _dot(a, b):
    return jnp.dot(a, b, preferred_element_type=F32)


def _dot_nt(a, b, precision=None):
    return lax.dot_general(a, b, (((1,), (1,)), ((), ())), precision=precision,
                           preferred_element_type=F32)


def _dot_tn(a, b):
    return lax.dot_general(a, b, (((0,), (0,)), ((), ())), preferred_element_type=F32)


def _layer_norm(y, g, b):
    mu = jnp.mean(y, axis=-1, keepdims=True)
    d = y - mu
    var = jnp.mean(d * d, axis=-1, keepdims=True)
    return d * lax.rsqrt(var + LN_EPS) * g + b


def _mm_kernel(a_ref, w_ref, o_ref):
    o_ref[...] = _dot(a_ref[...], w_ref[...]).astype(o_ref.dtype)


def _matmul(a, w, out_dtype, tm, tn):
    m, k = a.shape
    n = w.shape[1]
    return pl.pallas_call(
        _mm_kernel,
        out_shape=jax.ShapeDtypeStruct((m, n), out_dtype),
        grid=(m // tm, n // tn),
        in_specs=[pl.BlockSpec((tm, k), lambda i, j: (i, 0)),
                  pl.BlockSpec((k, tn), lambda i, j: (0, j))],
        out_specs=pl.BlockSpec((tm, tn), lambda i, j: (i, j)),
        compiler_params=_cparams(("parallel", "arbitrary")),
        name="matmul",
    )(a, w)


def _mm_res_ln_kernel(a_ref, w_ref, x_ref, g_ref, b_ref, o_ref, ob_ref):
    y = DN_ALPHA * x_ref[...] + _dot(a_ref[...], w_ref[...])
    out = _layer_norm(y, g_ref[...], b_ref[...])
    o_ref[...] = out
    ob_ref[...] = out.astype(BF16)


def _matmul_res_ln(a, w, x, g, b, tm=512):
    m, k = a.shape
    d = w.shape[1]
    return pl.pallas_call(
        _mm_res_ln_kernel,
        out_shape=(jax.ShapeDtypeStruct((m, d), F32), jax.ShapeDtypeStruct((m, d), BF16)),
        grid=(m // tm,),
        in_specs=[pl.BlockSpec((tm, k), lambda i: (i, 0)),
                  pl.BlockSpec((k, d), lambda i: (0, 0)),
                  pl.BlockSpec((tm, d), lambda i: (i, 0)),
                  pl.BlockSpec((1, d), lambda i: (0, 0)),
                  pl.BlockSpec((1, d), lambda i: (0, 0))],
        out_specs=(pl.BlockSpec((tm, d), lambda i: (i, 0)),
                   pl.BlockSpec((tm, d), lambda i: (i, 0))),
        compiler_params=_cparams(("parallel",)),
        name="matmul_res_ln",
    )(a, w, x, g.reshape(1, d), b.reshape(1, d))


def _dn_kernel(qkv_ref, z_ref, ab_ref, cw_ref, alog_ref, dtb_ref, ong_ref, o_ref,
               xc_ref, s_ref):
    t = DN_TILE
    dk = DN_DK
    qk_w = DN_HEADS * dk
    c_idx = pl.program_id(1)

    @pl.when(c_idx == 0)
    def _():
        xc_ref[0:8, :] = jnp.zeros((8, xc_ref.shape[1]), F32)
        s_ref[...] = jnp.zeros(s_ref.shape, F32)

    xc_ref[8:8 + t, :] = qkv_ref[...]

    ab = ab_ref[...]
    a_sh = ab + dtb_ref[...]
    softplus = jnp.maximum(a_sh, 0.0) + jnp.log(1.0 + jnp.exp(-jnp.abs(a_sh)))
    g_full = -jnp.exp(alog_ref[...]) * softplus
    beta_full = _sigmoid(ab)

    row = lax.broadcasted_iota(jnp.int32, (t, t), 0)
    col = lax.broadcasted_iota(jnp.int32, (t, t), 1)
    same = (row // CHUNK) == (col // CHUNK)
    m_incl = same & (row >= col)
    m_strict = same & (row > col)
    l_incl = jnp.where(m_incl, 1.0, 0.0).astype(F32)
    l_all = jnp.where(same, 1.0, 0.0).astype(F32)
    gc_all = jnp.dot(l_incl, g_full, precision=HIGHEST, preferred_element_type=F32)
    gl_all = jnp.dot(l_all, g_full, precision=HIGHEST, preferred_element_type=F32)
    gc_t = gc_all.T

    def conv_act(off):
        acc = xc_ref[5:5 + t, off:off + dk] * cw_ref[0:1, off:off + dk]
        for j in range(1, CONV_WIDTH):
            acc = acc + xc_ref[5 + j:5 + j + t, off:off + dk] * cw_ref[j:j + 1, off:off + dk]
        return acc * _sigmoid(acc)

    for h in range(DN_HEADS):
        q = conv_act(h * dk)
        k = conv_act(qk_w + h * dk)
        v = conv_act(2 * qk_w + h * dk)
        q = q * lax.rsqrt(jnp.sum(q * q, axis=-1, keepdims=True) + NORM_EPS) * (dk ** -0.5)
        k = k * lax.rsqrt(jnp.sum(k * k, axis=-1, keepdims=True) + NORM_EPS)
        beta = beta_full[:, DN_HEADS + h:DN_HEADS + h + 1]
        gc = gc_all[:, h:h + 1]
        gl = gl_all[:, h:h + 1]
        gr = gc_t[h:h + 1, :]
        decay = jnp.exp(jnp.where(m_incl, gc - gr, -jnp.inf))
        kb = k * beta
        k_b = k.astype(BF16)
        a_mat = jnp.where(m_strict, _dot_nt(kb.astype(BF16), k_b) * decay, 0.0)
        p_mat = _dot_nt(q.astype(BF16), k_b) * decay
        egc = jnp.exp(gc)
        y = jnp.concatenate([v * beta, kb * egc], axis=1)
        a_b = a_mat.astype(BF16)
        y = y - _dot(a_b, y.astype(BF16))
        for _ in range(5):
            a_b = _dot(a_b, a_b).astype(BF16)
            y = y + _dot(a_b, y.astype(BF16))
        u = y[:, :dk]
        w = y[:, dk:]
        qg = q * egc
        kd = k * jnp.exp(gl - gc)

        s0 = s_ref[h]
        r0 = _dot(jnp.concatenate([w[:CHUNK], qg[:CHUNK]], axis=0).astype(BF16), s0.astype(BF16))
        vn0 = u[:CHUNK] - r0[:CHUNK]
        s1 = s0 * jnp.exp(gl[0:1, :]) + _dot_tn(kd[:CHUNK].astype(BF16), vn0.astype(BF16))
        r1 = _dot(jnp.concatenate([w[CHUNK:], qg[CHUNK:]], axis=0).astype(BF16), s1.astype(BF16))
        vn1 = u[CHUNK:] - r1[:CHUNK]
        s2 = s1 * jnp.exp(gl[CHUNK:CHUNK + 1, :]) + _dot_tn(kd[CHUNK:].astype(BF16), vn1.astype(BF16))
        s_ref[h] = s2

        vn = jnp.concatenate([vn0, vn1], axis=0)
        o = jnp.concatenate([r0[CHUNK:], r1[CHUNK:]], axis=0) + _dot(p_mat.astype(BF16), vn.astype(BF16))
        o = o * lax.rsqrt(jnp.mean(o * o, axis=-1, keepdims=True) + NORM_EPS) * ong_ref[...]
        zh = z_ref[:, h * dk:(h + 1) * dk]
        o = o * (zh * _sigmoid(zh))
        o_ref[:, h * dk:(h + 1) * dk] = o.astype(o_ref.dtype)

    xc_ref[0:8, :] = xc_ref[t:t + 8, :]


def _deltanet(proj, ab, conv_w, a_log, dt_bias, out_norm_g, bsz, seq):
    n = proj.shape[0]
    t = DN_TILE
    nct = seq // t
    qkv_w = 3 * DN_HEADS * DN_DK
    v_w = DN_HEADS * DN_DK
    alog_p = jnp.zeros((1, LANES), F32).at[0, :DN_HEADS].set(a_log)
    dtb_p = jnp.zeros((1, LANES), F32).at[0, :DN_HEADS].set(dt_bias)
    return pl.pallas_call(
        _dn_kernel,
        out_shape=jax.ShapeDtypeStruct((n, v_w), BF16),
        grid=(bsz, nct),
        in_specs=[pl.BlockSpec((t, qkv_w), lambda b, c: (b * nct + c, 0)),
                  pl.BlockSpec((t, v_w), lambda b, c: (b * nct + c, qkv_w // v_w)),
                  pl.BlockSpec((t, LANES), lambda b, c: (b * nct + c, 0)),
                  pl.BlockSpec((CONV_WIDTH, qkv_w), lambda b, c: (0, 0)),
                  pl.BlockSpec((1, LANES), lambda b, c: (0, 0)),
                  pl.BlockSpec((1, LANES), lambda b, c: (0, 0)),
                  pl.BlockSpec((1, DN_DK), lambda b, c: (0, 0))],
        out_specs=pl.BlockSpec((t, v_w), lambda b, c: (b * nct + c, 0)),
        scratch_shapes=[pltpu.VMEM((t + 8, qkv_w), F32),
                        pltpu.VMEM((DN_HEADS, DN_DK, DN_DK), F32)],
        compiler_params=_cparams(("arbitrary", "arbitrary")),
        name="deltanet",
    )(proj, proj, ab, conv_w, alog_p, dtb_p, out_norm_g.reshape(1, DN_DK))


def _att_kernel(q_ref, kp_ref, kc_ref, vp_ref, vc_ref, bias_ref, o_ref, ks_ref, vs_ref):
    tq = ATT_TILE
    t_idx = pl.program_id(1)
    d = q_ref.shape[1]
    zeros = jnp.zeros((CHUNK, d), BF16)
    ks_ref[0:CHUNK, :] = zeros
    vs_ref[0:CHUNK, :] = zeros
    ks_ref[CHUNK:CHUNK + tq, :] = kp_ref[...]
    vs_ref[CHUNK:CHUNK + tq, :] = vp_ref[...]
    ks_ref[CHUNK + tq:CHUNK + 2 * tq, :] = kc_ref[...]
    vs_ref[CHUNK + tq:CHUNK + 2 * tq, :] = vc_ref[...]

    lane = lax.broadcasted_iota(jnp.int32, (CHUNK, 2 * ATT_DH), 1)
    first = lane < ATT_DH
    j_band = lax.broadcasted_iota(jnp.int32, (2 * CHUNK, ATT_BAND), 1)

    def chunk_body(c, carry):
        r0 = pl.multiple_of(c * CHUNK, CHUNK)
        kpos = t_idx * tq - (LEFT_CHUNKS + 1) * CHUNK + c * CHUNK + j_band
        valid = (j_band >= CHUNK) & (kpos >= 0)
        for hp in range(ATT_HEADS // 2):
            ls = slice(hp * 2 * ATT_DH, (hp + 1) * 2 * ATT_DH)
            qp = q_ref[pl.ds(r0, CHUNK), ls]
            zq = jnp.zeros_like(qp)
            q2 = jnp.concatenate([jnp.where(first, qp, zq), jnp.where(first, zq, qp)], axis=0)
            kb = ks_ref[pl.ds(r0, ATT_BAND), ls]
            vb = vs_ref[pl.ds(r0, ATT_BAND), ls]
            s = _dot_nt(q2, kb) * (ATT_DH ** -0.5)
            s = s + jnp.concatenate([bias_ref[2 * hp], bias_ref[2 * hp + 1]], axis=0)
            s = jnp.where(valid, s, -jnp.inf)
            m = jnp.max(s, axis=-1, keepdims=True)
            p = jnp.exp(s - m)
            l = jnp.sum(p, axis=-1, keepdims=True)
            p = p / l
            r = _dot(p.astype(BF16), vb)
            o_ref[pl.ds(r0, CHUNK), ls] = jnp.where(first, r[:CHUNK], r[CHUNK:]).astype(o_ref.dtype)
        return carry

    lax.fori_loop(0, tq // CHUNK, chunk_body, 0)


def _attention(q, kv, bias, bsz, seq):
    n, d = q.shape
    tq = ATT_TILE
    nt = seq // tq
    rows = CHUNK + 2 * tq
    prev = lambda b, t: (b * nt + jnp.maximum(t - 1, 0), 0)
    cur = lambda b, t: (b * nt + t, 0)
    prev_v = lambda b, t: (b * nt + jnp.maximum(t - 1, 0), 1)
    cur_v = lambda b, t: (b * nt + t, 1)
    return pl.pallas_call(
        _att_kernel,
        out_shape=jax.ShapeDtypeStruct((n, d), BF16),
        grid=(bsz, nt),
        in_specs=[pl.BlockSpec((tq, d), cur),
                  pl.BlockSpec((tq, d), prev),
                  pl.BlockSpec((tq, d), cur),
                  pl.BlockSpec((tq, d), prev_v),
                  pl.BlockSpec((tq, d), cur_v),
                  pl.BlockSpec((ATT_HEADS, CHUNK, ATT_BAND), lambda b, t: (0, 0, 0))],
        out_specs=pl.BlockSpec((tq, d), cur),
        scratch_shapes=[pltpu.VMEM((rows, d), BF16), pltpu.VMEM((rows, d), BF16)],
        compiler_params=_cparams(("parallel", "arbitrary")),
        name="band_attention",
    )(q, kv, kv, kv, kv, bias)


def _rel_bias(table):
    i_loc = jnp.arange(CHUNK)
    j_band = jnp.arange(ATT_BAND)
    rel = i_loc[:, None] + (LEFT_CHUNKS + 1) * CHUNK - j_band[None, :]
    rel_idx = jnp.clip(rel, -(CHUNK - 1), MAX_REL) + (CHUNK - 1)
    return table[:, rel_idx].astype(F32)


def _router_kernel(x_ref, wt_ref, bias_ref, idx_ref, gate_ref, rank_ref, cnt_ref):
    tt = x_ref.shape[0]
    ne = wt_ref.shape[0]
    gsz = ne // N_GROUPS

    @pl.when(pl.program_id(0) == 0)
    def _():
        cnt_ref[...] = jnp.zeros(cnt_ref.shape, F32)

    logits = _dot_nt(wt_ref[...], x_ref[...], precision=HIGHEST)
    scores = _sigmoid(logits)
    sel = scores + bias_ref[...]
    neg = -jnp.inf

    iota_g = lax.broadcasted_iota(jnp.int32, (gsz, tt), 0)
    gs_rows = []
    for g in range(N_GROUPS):
        blk = sel[g * gsz:(g + 1) * gsz, :]
        m1 = jnp.max(blk, axis=0, keepdims=True)
        i1 = jnp.min(jnp.where(blk == m1, iota_g, gsz), axis=0, keepdims=True)
        m2 = jnp.max(jnp.where(iota_g == i1, neg, blk), axis=0, keepdims=True)
        gs_rows.append(m1 + m2)
    gsc = jnp.concatenate(gs_rows, axis=0)
    iota_n = lax.broadcasted_iota(jnp.int32, (N_GROUPS, tt), 0)
    chosen = jnp.zeros((N_GROUPS, tt), F32)
    for _ in range(TOPK_GROUPS):
        m = jnp.max(gsc, axis=0, keepdims=True)
        i = jnp.min(jnp.where(gsc == m, iota_n, N_GROUPS), axis=0, keepdims=True)
        hit = iota_n == i
        chosen = jnp.where(hit, 1.0, chosen)
        gsc = jnp.where(hit, neg, gsc)
    selm = jnp.concatenate(
        [jnp.where(chosen[g:g + 1, :] > 0.0, sel[g * gsz:(g + 1) * gsz, :], neg)
         for g in range(N_GROUPS)], axis=0)

    iota_e = lax.broadcasted_iota(jnp.int32, (ne, tt), 0)
    idx_rows, w_rows, hits = [], [], []
    for _ in range(TOP_K):
        m = jnp.max(selm, axis=0, keepdims=True)
        i = jnp.min(jnp.where(selm == m, iota_e, ne), axis=0, keepdims=True)
        hit = iota_e == i
        w_rows.append(jnp.sum(jnp.where(hit, scores, 0.0), axis=0, keepdims=True))
        idx_rows.append(i)
        hits.append(hit)
        selm = jnp.where(hit, neg, selm)
    wts = jnp.concatenate(w_rows, axis=0)
    wts = wts / jnp.sum(wts, axis=0, keepdims=True) * ROUTED_SCALE
    idx_ref[...] = jnp.concatenate(idx_rows, axis=0)
    gate_ref[...] = wts

    onehot = jnp.where(hits[0], 1.0, 0.0)
    for hit in hits[1:]:
        onehot = onehot + jnp.where(hit, 1.0, 0.0)
    onehot = onehot.astype(BF16)
    t_row = lax.broadcasted_iota(jnp.int32, (tt, tt), 0)
    t_col = lax.broadcasted_iota(jnp.int32, (tt, tt), 1)
    before = jnp.where(t_row < t_col, 1.0, 0.0).astype(BF16)
    pos = cnt_ref[:, 0:1] + _dot(onehot, before)
    rank_rows = [jnp.sum(jnp.where(hit, pos, 0.0), axis=0, keepdims=True) for hit in hits]
    rank_ref[...] = jnp.concatenate(rank_rows, axis=0).astype(jnp.int32)
    cnt_ref[...] = cnt_ref[...] + _dot(onehot, jnp.ones((tt, LANES), BF16))


def _router(x, w_router, router_bias):
    n, d = x.shape
    ne = w_router.shape[1]
    tt = ROUTER_TILE
    return pl.pallas_call(
        _router_kernel,
        out_shape=(jax.ShapeDtypeStruct((TOP_K, n), jnp.int32),
                   jax.ShapeDtypeStruct((TOP_K, n), F32),
                   jax.ShapeDtypeStruct((TOP_K, n), jnp.int32),
                   jax.ShapeDtypeStruct((ne, LANES), F32)),
        grid=(n // tt,),
        in_specs=[pl.BlockSpec((tt, d), lambda i: (i, 0)),
                  pl.BlockSpec((ne, d), lambda i: (0, 0)),
                  pl.BlockSpec((ne, 1), lambda i: (0, 0))],
        out_specs=(pl.BlockSpec((TOP_K, tt), lambda i: (0, i)),
                   pl.BlockSpec((TOP_K, tt), lambda i: (0, i)),
                   pl.BlockSpec((TOP_K, tt), lambda i: (0, i)),
                   pl.BlockSpec((ne, LANES), lambda i: (0, 0))),
        compiler_params=_cparams(("arbitrary",)),
        name="router",
    )(x, w_router.T, router_bias.reshape(ne, 1))


def _dest_kernel(idx_ref, rank_ref, pstart_ref, dest_ref):
    ne = pstart_ref.shape[0]
    tt = idx_ref.shape[1]
    iota_e = lax.broadcasted_iota(jnp.int32, (ne, tt), 0)
    pstart = pstart_ref[...]
    idx = idx_ref[...]
    rows = [jnp.sum(jnp.where(iota_e == idx[k:k + 1, :], pstart, 0), axis=0, keepdims=True)
            for k in range(TOP_K)]
    dest_ref[0] = jnp.concatenate(rows, axis=0) + rank_ref[...]


def _dest_rows(eidx_t, rank_t, pstart):
    k, n = eidx_t.shape
    ne = pstart.shape[0]
    tt = DISPATCH_TILE
    return pl.pallas_call(
        _dest_kernel,
        out_shape=jax.ShapeDtypeStruct((n // tt, k, tt), jnp.int32),
        grid=(n // tt,),
        in_specs=[pl.BlockSpec((k, tt), lambda i: (0, i)),
                  pl.BlockSpec((k, tt), lambda i: (0, i)),
                  pl.BlockSpec((ne, 1), lambda i: (0, 0))],
        out_specs=pl.BlockSpec((1, k, tt), lambda i: (i, 0, 0)),
        compiler_params=_cparams(("parallel",)),
        name="dest_rows",
    )(eidx_t, rank_t, pstart.reshape(ne, 1))


def _dispatch_kernel(dest_ref, x_ref, xs_ref, sem):
    tm = x_ref.shape[0]

    def row_copy(t, k):
        return pltpu.make_async_copy(x_ref.at[pl.ds(t, 1)],
                                     xs_ref.at[pl.ds(dest_ref[0, k, t], 1)], sem)

    def start(t, carry):
        for k in range(TOP_K):
            row_copy(t, k).start()
        return carry

    def wait(t, carry):
        for k in range(TOP_K):
            row_copy(t, k).wait()
        return carry

    lax.fori_loop(0, tm, start, 0)
    lax.fori_loop(0, tm, wait, 0)


def _dispatch(x, dest3, p_rows):
    n, d = x.shape
    nt, k, tm = dest3.shape
    return pl.pallas_call(
        _dispatch_kernel,
        out_shape=jax.ShapeDtypeStruct((p_rows, d), x.dtype),
        grid=(nt,),
        in_specs=[pl.BlockSpec((1, k, tm), lambda i: (i, 0, 0), memory_space=pltpu.SMEM),
                  pl.BlockSpec((tm, d), lambda i: (i, 0))],
        out_specs=pl.BlockSpec(memory_space=pl.ANY),
        scratch_shapes=[pltpu.SemaphoreType.DMA(())],
        compiler_params=_cparams(("arbitrary",)),
        name="dispatch",
    )(dest3, x)


def _expert_kernel(be_ref, nv_ref, x_ref, wgu_ref, wd_ref, y_ref):
    i = pl.program_id(0)
    f = wd_ref.shape[1]

    @pl.when(i < nv_ref[0])
    def _():
        h = _dot(x_ref[...].astype(BF16), wgu_ref[0].astype(BF16))
        g = h[:, :f]
        act = g * _sigmoid(g) * h[:, f:]
        y_ref[...] = _dot(act.astype(BF16), wd_ref[0].astype(BF16))

    @pl.when(i >= nv_ref[0])
    def _():
        y_ref[...] = jnp.zeros(y_ref.shape, y_ref.dtype)


def _experts(xs, block_e, n_valid_blocks, w_gate_up, w_down):
    p, d = xs.shape
    nblk = p // MOE_BLK
    f2 = w_gate_up.shape[2]
    f = w_down.shape[1]
    grid_spec = pltpu.PrefetchScalarGridSpec(
        num_scalar_prefetch=2,
        grid=(nblk,),
        in_specs=[pl.BlockSpec((MOE_BLK, d), lambda i, be, nv: (jnp.minimum(i, nv[0] - 1), 0)),
                  pl.BlockSpec((1, d, f2), lambda i, be, nv: (be[i], 0, 0)),
                  pl.BlockSpec((1, f, d), lambda i, be, nv: (be[i], 0, 0))],
        out_specs=pl.BlockSpec((MOE_BLK, d), lambda i, be, nv: (i, 0)),
    )
    return pl.pallas_call(
        _expert_kernel,
        out_shape=jax.ShapeDtypeStruct((p, d), F32),
        grid_spec=grid_spec,
        compiler_params=_cparams(("arbitrary",)),
        name="routed_experts",
    )(block_e, n_valid_blocks, xs, w_gate_up, w_down)


def _moe_final_kernel(dest_ref, x_ref, gate_ref, ys_ref, wsg_ref, wsd_ref, g_ref, b_ref,
                      o_ref, ob_ref, ybuf_ref, sem):
    tm = x_ref.shape[0]
    f = wsd_ref.shape[0]

    def row_copy(t, k):
        return pltpu.make_async_copy(ys_ref.at[pl.ds(dest_ref[0, k, t], 1)],
                                     ybuf_ref.at[k, pl.ds(t, 1)], sem)

    def start(t, carry):
        for k in range(TOP_K):
            row_copy(t, k).start()
        return carry

    def wait(t, carry):
        for k in range(TOP_K):
            row_copy(t, k).wait()
        return carry

    lax.fori_loop(0, tm, start, 0)
    x = x_ref[...]
    h = _dot(x.astype(BF16), wsg_ref[...])
    g = h[:, :f]
    acc = _dot((g * _sigmoid(g) * h[:, f:]).astype(BF16), wsd_ref[...])
    lax.fori_loop(0, tm, wait, 0)
    gates = gate_ref[...]
    for k in range(TOP_K):
        acc = acc + ybuf_ref[k] * gates[:, k:k + 1]
    out = _layer_norm(DN_ALPHA * x + acc, g_ref[...], b_ref[...])
    o_ref[...] = out
    ob_ref[...] = out.astype(BF16)


def _moe_final(x, dest3, gates, ys, w_sgu, w_sd, g, b):
    n, d = x.shape
    nt, k, tm = dest3.shape
    return pl.pallas_call(
        _moe_final_kernel,
        out_shape=(jax.ShapeDtypeStruct((n, d), F32), jax.ShapeDtypeStruct((n, d), BF16)),
        grid=(nt,),
        in_specs=[pl.BlockSpec((1, k, tm), lambda i: (i, 0, 0), memory_space=pltpu.SMEM),
                  pl.BlockSpec((tm, d), lambda i: (i, 0)),
                  pl.BlockSpec((tm, k), lambda i: (i, 0)),
                  pl.BlockSpec(memory_space=pl.ANY),
                  pl.BlockSpec(w_sgu.shape, lambda i: (0, 0)),
                  pl.BlockSpec(w_sd.shape, lambda i: (0, 0)),
                  pl.BlockSpec((1, d), lambda i: (0, 0)),
                  pl.BlockSpec((1, d), lambda i: (0, 0))],
        out_specs=(pl.BlockSpec((tm, d), lambda i: (i, 0)),
                   pl.BlockSpec((tm, d), lambda i: (i, 0))),
        scratch_shapes=[pltpu.VMEM((k, tm, d), ys.dtype), pltpu.SemaphoreType.DMA(())],
        compiler_params=_cparams(("arbitrary",)),
        name="moe_final",
    )(dest3, x, gates, ys, w_sgu, w_sd, g.reshape(1, d), b.reshape(1, d))


def _block_plan(counts, n_assign):
    ne = counts.shape[0]
    nblk = (n_assign + ne * MOE_BLK) // MOE_BLK
    padded = (counts + MOE_BLK - 1) // MOE_BLK * MOE_BLK
    pend = jnp.cumsum(padded)
    pstart = pend - padded
    n_valid = pend[-1] // MOE_BLK
    blk_start = jnp.arange(nblk, dtype=jnp.int32) * MOE_BLK
    block_e = jnp.minimum(jnp.searchsorted(pend, blk_start, side='right'), ne - 1).astype(jnp.int32)
    last_e = block_e[jnp.maximum(n_valid - 1, 0)]
    block_e = jnp.where(jnp.arange(nblk) < n_valid, block_e, last_e)
    return pstart.astype(jnp.int32), block_e, n_valid.astype(jnp.int32).reshape(1), nblk * MOE_BLK


def _moe_layer(x_f32, w_router, router_bias, w_gate_up, w_down, w_sgu, w_sd, ln_g, ln_b):
    eidx_t, gates_t, rank_t, cnt = _router(x_f32, w_router, router_bias)
    counts = cnt[:, 0].astype(jnp.int32)
    pstart, block_e, n_valid, p_rows = _block_plan(counts, eidx_t.shape[0] * eidx_t.shape[1])
    dest3 = _dest_rows(eidx_t, rank_t, pstart)
    xs = _dispatch(x_f32, dest3, p_rows)
    ys = _experts(xs, block_e, n_valid, w_gate_up, w_down)
    return _moe_final(x_f32, dest3, gates_t.T, ys, w_sgu.astype(BF16), w_sd.astype(BF16), ln_g, ln_b)


def kernel(x, a_w_in, a_conv_w, a_a_log, a_dt_bias, a_out_norm_g, a_w_out, w_kv_shared, b_w_q, b_rel_bias, b_w_out, moe_w_router, moe_router_bias, moe_w_gate_up, moe_w_down, moe_w_shared_gate_up, moe_w_shared_down, ln_mix_g, ln_mix_b, ln_ffn_g, ln_ffn_b):
    bsz, seq, d = x.shape
    n = bsz * seq
    xf = x.reshape(n, d)
    xb = xf.astype(BF16)

    main_w = 4 * DN_HEADS * DN_DK
    w_in = a_w_in[0]
    proj = _matmul(xb, w_in[:, :main_w].astype(BF16), F32, 512, 1024)
    w_ab = jnp.zeros((d, LANES), F32).at[:, :2 * DN_HEADS].set(w_in[:, main_w:]).astype(BF16)
    ab = _matmul(xb, w_ab, F32, 512, LANES)
    o = _deltanet(proj, ab, a_conv_w[0], a_a_log[0], a_dt_bias[0], a_out_norm_g[0], bsz, seq)
    x1, _ = _matmul_res_ln(o, a_w_out[0].astype(BF16), xf, ln_mix_g[0], ln_mix_b[0])
    x2, x2b = _moe_layer(x1, moe_w_router[0], moe_router_bias[0], moe_w_gate_up[0], moe_w_down[0],
                         moe_w_shared_gate_up[0], moe_w_shared_down[0], ln_ffn_g[0], ln_ffn_b[0])

    kv = _matmul(x2b, w_kv_shared.astype(BF16), BF16, 512, 1024)
    q = _matmul(x2b, b_w_q[0].astype(BF16), BF16, 512, 1024)
    att = _attention(q, kv, _rel_bias(b_rel_bias[0]), bsz, seq)
    x3, _ = _matmul_res_ln(att, b_w_out[0].astype(BF16), x2, ln_mix_g[1], ln_mix_b[1])
    x4, _ = _moe_layer(x3, moe_w_router[1], moe_router_bias[1], moe_w_gate_up[1], moe_w_down[1],
                       moe_w_shared_gate_up[1], moe_w_shared_down[1], ln_ffn_g[1], ln_ffn_b[1])
    return x4.reshape(bsz, seq, d)
```

```python
import jax
import jax.numpy as jnp
from jax import lax
from jax.experimental import pallas as pl
from jax.experimental.pallas import tpu as pltpu

F32 = jnp.float32
BF16 = jnp.bfloat16
HIGHEST = lax.Precision.HIGHEST

CHUNK = 64
DN_HEADS = 8
DN_DK = 128
CONV_WIDTH = 4
ATT_HEADS = 16
ATT_DH = 64
LEFT_CHUNKS = 8
MAX_REL = 256
N_EXPERTS = 256
TOP_K = 8
N_GROUPS = 8
TOPK_GROUPS = 4
ROUTED_SCALE = 2.5
DEPTH = 2
DN_ALPHA = (2 * DEPTH) ** 0.25
LN_EPS = 1e-5
NORM_EPS = 1e-6

LANES = 128
VMEM_LIMIT = 56 * 1024 * 1024

DN_TILE = 2 * CHUNK
ATT_TILE = 8 * CHUNK
ATT_BAND = (LEFT_CHUNKS + 2) * CHUNK
MOE_BLK = 256
DISPATCH_TILE = 256
ROUTER_TILE = 512


def _cparams(sem):
    return pltpu.CompilerParams(dimension_semantics=sem, vmem_limit_bytes=VMEM_LIMIT)


def _sigmoid(x):
    return 1.0 / (1.0 + jnp.exp(-x))


def _dot(a, b):
    return jnp.dot(a, b, preferred_element_type=F32)


def _dot_nt(a, b, precision=None):
    return lax.dot_general(a, b, (((1,), (1,)), ((), ())), precision=precision,
                           preferred_element_type=F32)


def _dot_tn(a, b):
    return lax.dot_general(a, b, (((0,), (0,)), ((), ())), preferred_element_type=F32)


def _layer_norm(y, g, b):
    mu = jnp.mean(y, axis=-1, keepdims=True)
    d = y - mu
    var = jnp.mean(d * d, axis=-1, keepdims=True)
    return d * lax.rsqrt(var + LN_EPS) * g + b


def _mm_kernel(a_ref, w_ref, o_ref):
    o_ref[...] = _dot(a_ref[...], w_ref[...]).astype(o_ref.dtype)


def _matmul(a, w, out_dtype, tm, tn):
    m, k = a.shape
    n = w.shape[1]
    return pl.pallas_call(
        _mm_kernel,
        out_shape=jax.ShapeDtypeStruct((m, n), out_dtype),
        grid=(m // tm, n // tn),
        in_specs=[pl.BlockSpec((tm, k), lambda i, j: (i, 0)),
                  pl.BlockSpec((k, tn), lambda i, j: (0, j))],
        out_specs=pl.BlockSpec((tm, tn), lambda i, j: (i, j)),
        compiler_params=_cparams(("parallel", "arbitrary")),
        name="matmul",
    )(a, w)


def _mm_res_ln_kernel(a_ref, w_ref, x_ref, g_ref, b_ref, o_ref, ob_ref):
    y = DN_ALPHA * x_ref[...] + _dot(a_ref[...], w_ref[...])
    out = _layer_norm(y, g_ref[...], b_ref[...])
    o_ref[...] = out
    ob_ref[...] = out.astype(BF16)


def _matmul_res_ln(a, w, x, g, b, tm=512):
    m, k = a.shape
    d = w.shape[1]
    return pl.pallas_call(
        _mm_res_ln_kernel,
        out_shape=(jax.ShapeDtypeStruct((m, d), F32), jax.ShapeDtypeStruct((m, d), BF16)),
        grid=(m // tm,),
        in_specs=[pl.BlockSpec((tm, k), lambda i: (i, 0)),
                  pl.BlockSpec((k, d), lambda i: (0, 0)),
                  pl.BlockSpec((tm, d), lambda i: (i, 0)),
                  pl.BlockSpec((1, d), lambda i: (0, 0)),
                  pl.BlockSpec((1, d), lambda i: (0, 0))],
        out_specs=(pl.BlockSpec((tm, d), lambda i: (i, 0)),
                   pl.BlockSpec((tm, d), lambda i: (i, 0))),
        compiler_params=_cparams(("parallel",)),
        name="matmul_res_ln",
    )(a, w, x, g.reshape(1, d), b.reshape(1, d))


def _dn_kernel(qkv_ref, z_ref, ab_ref, cw_ref, alog_ref, dtb_ref, ong_ref, o_ref,
               xc_ref, s_ref):
    t = DN_TILE
    dk = DN_DK
    qk_w = DN_HEADS * dk
    c_idx = pl.program_id(1)

    @pl.when(c_idx == 0)
    def _():
        xc_ref[0:8, :] = jnp.zeros((8, xc_ref.shape[1]), F32)
        s_ref[...] = jnp.zeros(s_ref.shape, F32)

    xc_ref[8:8 + t, :] = qkv_ref[...]

    ab = ab_ref[...]
    a_sh = ab + dtb_ref[...]
    softplus = jnp.maximum(a_sh, 0.0) + jnp.log(1.0 + jnp.exp(-jnp.abs(a_sh)))
    g_full = -jnp.exp(alog_ref[...]) * softplus
    beta_full = _sigmoid(ab)

    row = lax.broadcasted_iota(jnp.int32, (t, t), 0)
    col = lax.broadcasted_iota(jnp.int32, (t, t), 1)
    same = (row // CHUNK) == (col // CHUNK)
    m_incl = same & (row >= col)
    m_strict = same & (row > col)
    l_incl = jnp.where(m_incl, 1.0, 0.0).astype(F32)
    l_all = jnp.where(same, 1.0, 0.0).astype(F32)
    gc_all = jnp.dot(l_incl, g_full, precision=HIGHEST, preferred_element_type=F32)
    gl_all = jnp.dot(l_all, g_full, precision=HIGHEST, preferred_element_type=F32)
    gc_t = gc_all.T

    def conv_act(off):
        acc = xc_ref[5:5 + t, off:off + dk] * cw_ref[0:1, off:off + dk]
        for j in range(1, CONV_WIDTH):
            acc = acc + xc_ref[5 + j:5 + j + t, off:off + dk] * cw_ref[j:j + 1, off:off + dk]
        return acc * _sigmoid(acc)

    hs = range(DN_HEADS)
    q, k, kb_, egc, gls, gcs = [], [], [], [], [], []
    a_b, p_mat, y = [], [], []
    for h in hs:
        qh = conv_act(h * dk)
        kh = conv_act(qk_w + h * dk)
        vh = conv_act(2 * qk_w + h * dk)
        qh = qh * lax.rsqrt(jnp.sum(qh * qh, axis=-1, keepdims=True) + NORM_EPS) * (dk ** -0.5)
        kh = kh * lax.rsqrt(jnp.sum(kh * kh, axis=-1, keepdims=True) + NORM_EPS)
        beta = beta_full[:, DN_HEADS + h:DN_HEADS + h + 1]
        gc = gc_all[:, h:h + 1]
        gr = gc_t[h:h + 1, :]
        decay = jnp.exp(jnp.where(m_incl, gc - gr, -jnp.inf))
        kb = kh * beta
        k_b = kh.astype(BF16)
        a_b.append(jnp.where(m_strict, _dot_nt(kb.astype(BF16), k_b) * decay, 0.0).astype(BF16))
        p_mat.append((_dot_nt(qh.astype(BF16), k_b) * decay).astype(BF16))
        e = jnp.exp(gc)
        y.append(jnp.concatenate([vh * beta, kb * e], axis=1))
        q.append(qh)
        k.append(kh)
        egc.append(e)
        gcs.append(gc)
        gls.append(gl_all[:, h:h + 1])

    y = [y[h] - _dot(a_b[h], y[h].astype(BF16)) for h in hs]
    for _ in range(5):
        a_b = [_dot(a_b[h], a_b[h]).astype(BF16) for h in hs]
        y = [y[h] + _dot(a_b[h], y[h].astype(BF16)) for h in hs]

    wq0, wq1, kd0, kd1 = [], [], [], []
    for h in hs:
        w = y[h][:, dk:]
        qg = q[h] * egc[h]
        kd = (k[h] * jnp.exp(gls[h] - gcs[h])).astype(BF16)
        wq0.append(jnp.concatenate([w[:CHUNK], qg[:CHUNK]], axis=0).astype(BF16))
        wq1.append(jnp.concatenate([w[CHUNK:], qg[CHUNK:]], axis=0).astype(BF16))
        kd0.append(kd[:CHUNK])
        kd1.append(kd[CHUNK:])

    s0 = [s_ref[h] for h in hs]
    r0 = [_dot(wq0[h], s0[h].astype(BF16)) for h in hs]
    vn0 = [y[h][:CHUNK, :dk] - r0[h][:CHUNK] for h in hs]
    s1 = [s0[h] * jnp.exp(gls[h][0:1, :]) + _dot_tn(kd0[h], vn0[h].astype(BF16)) for h in hs]
    r1 = [_dot(wq1[h], s1[h].astype(BF16)) for h in hs]
    vn1 = [y[h][CHUNK:, :dk] - r1[h][:CHUNK] for h in hs]
    for h in hs:
        s_ref[h] = (s1[h] * jnp.exp(gls[h][CHUNK:CHUNK + 1, :])
                    + _dot_tn(kd1[h], vn1[h].astype(BF16)))
    for h in hs:
        vn = jnp.concatenate([vn0[h], vn1[h]], axis=0).astype(BF16)
        o = jnp.concatenate([r0[h][CHUNK:], r1[h][CHUNK:]], axis=0) + _dot(p_mat[h], vn)
        o = o * lax.rsqrt(jnp.mean(o * o, axis=-1, keepdims=True) + NORM_EPS) * ong_ref[...]
        zh = z_ref[:, h * dk:(h + 1) * dk]
        o = o * (zh * _sigmoid(zh))
        o_ref[:, h * dk:(h + 1) * dk] = o.astype(o_ref.dtype)

    xc_ref[0:8, :] = xc_ref[t:t + 8, :]


def _deltanet(proj, ab, conv_w, a_log, dt_bias, out_norm_g, bsz, seq):
    n = proj.shape[0]
    t = DN_TILE
    nct = seq // t
    qkv_w = 3 * DN_HEADS * DN_DK
    v_w = DN_HEADS * DN_DK
    alog_p = jnp.zeros((1, LANES), F32).at[0, :DN_HEADS].set(a_log)
    dtb_p = jnp.zeros((1, LANES), F32).at[0, :DN_HEADS].set(dt_bias)
    return pl.pallas_call(
        _dn_kernel,
        out_shape=jax.ShapeDtypeStruct((n, v_w), BF16),
        grid=(bsz, nct),
        in_specs=[pl.BlockSpec((t, qkv_w), lambda b, c: (b * nct + c, 0)),
                  pl.BlockSpec((t, v_w), lambda b, c: (b * nct + c, qkv_w // v_w)),
                  pl.BlockSpec((t, LANES), lambda b, c: (b * nct + c, 0)),
                  pl.BlockSpec((CONV_WIDTH, qkv_w), lambda b, c: (0, 0)),
                  pl.BlockSpec((1, LANES), lambda b, c: (0, 0)),
                  pl.BlockSpec((1, LANES), lambda b, c: (0, 0)),
                  pl.BlockSpec((1, DN_DK), lambda b, c: (0, 0))],
        out_specs=pl.BlockSpec((t, v_w), lambda b, c: (b * nct + c, 0)),
        scratch_shapes=[pltpu.VMEM((t + 8, qkv_w), F32),
                        pltpu.VMEM((DN_HEADS, DN_DK, DN_DK), F32)],
        compiler_params=_cparams(("arbitrary", "arbitrary")),
        name="deltanet",
    )(proj, proj, ab, conv_w, alog_p, dtb_p, out_norm_g.reshape(1, DN_DK))


def _att_kernel(q_ref, kp_ref, kc_ref, vp_ref, vc_ref, bias_ref, o_ref, ks_ref, vs_ref):
    tq = ATT_TILE
    t_idx = pl.program_id(1)
    d = q_ref.shape[1]
    zeros = jnp.zeros((CHUNK, d), BF16)
    ks_ref[0:CHUNK, :] = zeros
    vs_ref[0:CHUNK, :] = zeros
    ks_ref[CHUNK:CHUNK + tq, :] = kp_ref[...]
    vs_ref[CHUNK:CHUNK + tq, :] = vp_ref[...]
    ks_ref[CHUNK + tq:CHUNK + 2 * tq, :] = kc_ref[...]
    vs_ref[CHUNK + tq:CHUNK + 2 * tq, :] = vc_ref[...]

    lane = lax.broadcasted_iota(jnp.int32, (CHUNK, 2 * ATT_DH), 1)
    first = lane < ATT_DH
    j_band = lax.broadcasted_iota(jnp.int32, (2 * CHUNK, ATT_BAND), 1)

    def chunk_body(c, carry):
        r0 = pl.multiple_of(c * CHUNK, CHUNK)
        kpos = t_idx * tq - (LEFT_CHUNKS + 1) * CHUNK + c * CHUNK + j_band
        valid = (j_band >= CHUNK) & (kpos >= 0)
        pairs = range(ATT_HEADS // 2)
        lanes = [slice(hp * 2 * ATT_DH, (hp + 1) * 2 * ATT_DH) for hp in pairs]
        scores = []
        for hp in pairs:
            qp = q_ref[pl.ds(r0, CHUNK), lanes[hp]] * jnp.asarray(ATT_DH ** -0.5, BF16)
            zq = jnp.zeros_like(qp)
            q2 = jnp.concatenate([jnp.where(first, qp, zq), jnp.where(first, zq, qp)], axis=0)
            s = _dot_nt(q2, ks_ref[pl.ds(r0, ATT_BAND), lanes[hp]])
            s = s + jnp.concatenate([bias_ref[2 * hp], bias_ref[2 * hp + 1]], axis=0)
            scores.append(jnp.where(valid, s, -jnp.inf))
        probs, denom = [], []
        for hp in pairs:
            m = jnp.max(scores[hp], axis=-1, keepdims=True)
            p = jnp.exp(scores[hp] - m)
            denom.append(jnp.sum(p, axis=-1, keepdims=True))
            probs.append(p.astype(BF16))
        for hp in pairs:
            r = _dot(probs[hp], vs_ref[pl.ds(r0, ATT_BAND), lanes[hp]]) / denom[hp]
            o_ref[pl.ds(r0, CHUNK), lanes[hp]] = jnp.where(first, r[:CHUNK], r[CHUNK:]).astype(o_ref.dtype)
        return carry

    lax.fori_loop(0, tq // CHUNK, chunk_body, 0)


def _attention(q, kv, bias, bsz, seq):
    n, d = q.shape
    tq = ATT_TILE
    nt = seq // tq
    rows = CHUNK + 2 * tq
    prev = lambda b, t: (b * nt + jnp.maximum(t - 1, 0), 0)
    cur = lambda b, t: (b * nt + t, 0)
    prev_v = lambda b, t: (b * nt + jnp.maximum(t - 1, 0), 1)
    cur_v = lambda b, t: (b * nt + t, 1)
    return pl.pallas_call(
        _att_kernel,
        out_shape=jax.ShapeDtypeStruct((n, d), BF16),
        grid=(bsz, nt),
        in_specs=[pl.BlockSpec((tq, d), cur),
                  pl.BlockSpec((tq, d), prev),
                  pl.BlockSpec((tq, d), cur),
                  pl.BlockSpec((tq, d), prev_v),
                  pl.BlockSpec((tq, d), cur_v),
                  pl.BlockSpec((ATT_HEADS, CHUNK, ATT_BAND), lambda b, t: (0, 0, 0))],
        out_specs=pl.BlockSpec((tq, d), cur),
        scratch_shapes=[pltpu.VMEM((rows, d), BF16), pltpu.VMEM((rows, d), BF16)],
        compiler_params=_cparams(("parallel", "arbitrary")),
        name="band_attention",
    )(q, kv, kv, kv, kv, bias)


def _rel_bias(table):
    i_loc = jnp.arange(CHUNK)
    j_band = jnp.arange(ATT_BAND)
    rel = i_loc[:, None] + (LEFT_CHUNKS + 1) * CHUNK - j_band[None, :]
    rel_idx = jnp.clip(rel, -(CHUNK - 1), MAX_REL) + (CHUNK - 1)
    return table[:, rel_idx].astype(F32)


def _router_kernel(x_ref, wt_ref, bias_ref, idx_ref, gate_ref, rank_ref, cnt_ref):
    tt = x_ref.shape[0]
    ne = wt_ref.shape[0]
    gsz = ne // N_GROUPS

    @pl.when(pl.program_id(0) == 0)
    def _():
        cnt_ref[...] = jnp.zeros(cnt_ref.shape, F32)

    logits = _dot_nt(wt_ref[...], x_ref[...], precision=HIGHEST)
    scores = _sigmoid(logits)
    sel = scores + bias_ref[...]
    neg = -jnp.inf

    iota_g = lax.broadcasted_iota(jnp.int32, (gsz, tt), 0)
    gs_rows = []
    for g in range(N_GROUPS):
        blk = sel[g * gsz:(g + 1) * gsz, :]
        m1 = jnp.max(blk, axis=0, keepdims=True)
        i1 = jnp.min(jnp.where(blk == m1, iota_g, gsz), axis=0, keepdims=True)
        m2 = jnp.max(jnp.where(iota_g == i1, neg, blk), axis=0, keepdims=True)
        gs_rows.append(m1 + m2)
    gsc = jnp.concatenate(gs_rows, axis=0)
    iota_n = lax.broadcasted_iota(jnp.int32, (N_GROUPS, tt), 0)
    chosen = jnp.zeros((N_GROUPS, tt), F32)
    for _ in range(TOPK_GROUPS):
        m = jnp.max(gsc, axis=0, keepdims=True)
        i = jnp.min(jnp.where(gsc == m, iota_n, N_GROUPS), axis=0, keepdims=True)
        hit = iota_n == i
        chosen = jnp.where(hit, 1.0, chosen)
        gsc = jnp.where(hit, neg, gsc)
    selm = jnp.concatenate(
        [jnp.where(chosen[g:g + 1, :] > 0.0, sel[g * gsz:(g + 1) * gsz, :], neg)
         for g in range(N_GROUPS)], axis=0)

    iota_e = lax.broadcasted_iota(jnp.int32, (ne, tt), 0)
    idx_rows, w_rows, hits = [], [], []
    for _ in range(TOP_K):
        m = jnp.max(selm, axis=0, keepdims=True)
        i = jnp.min(jnp.where(selm == m, iota_e, ne), axis=0, keepdims=True)
        hit = iota_e == i
        w_rows.append(jnp.sum(jnp.where(hit, scores, 0.0), axis=0, keepdims=True))
        idx_rows.append(i)
        hits.append(hit)
        selm = jnp.where(hit, neg, selm)
    wts = jnp.concatenate(w_rows, axis=0)
    wts = wts / jnp.sum(wts, axis=0, keepdims=True) * ROUTED_SCALE
    idx_ref[...] = jnp.concatenate(idx_rows, axis=0)
    gate_ref[...] = wts

    onehot = jnp.where(hits[0], 1.0, 0.0)
    for hit in hits[1:]:
        onehot = onehot + jnp.where(hit, 1.0, 0.0)
    onehot = onehot.astype(BF16)
    t_row = lax.broadcasted_iota(jnp.int32, (tt, tt), 0)
    t_col = lax.broadcasted_iota(jnp.int32, (tt, tt), 1)
    before = jnp.where(t_row < t_col, 1.0, 0.0).astype(BF16)
    pos = cnt_ref[:, 0:1] + _dot(onehot, before)
    rank_rows = [jnp.sum(jnp.where(hit, pos, 0.0), axis=0, keepdims=True) for hit in hits]
    rank_ref[...] = jnp.concatenate(rank_rows, axis=0).astype(jnp.int32)
    cnt_ref[...] = cnt_ref[...] + _dot(onehot, jnp.ones((tt, LANES), BF16))


def _router(x, w_router, router_bias):
    n, d = x.shape
    ne = w_router.shape[1]
    tt = ROUTER_TILE
    return pl.pallas_call(
        _router_kernel,
        out_shape=(jax.ShapeDtypeStruct((TOP_K, n), jnp.int32),
                   jax.ShapeDtypeStruct((TOP_K, n), F32),
                   jax.ShapeDtypeStruct((TOP_K, n), jnp.int32),
                   jax.ShapeDtypeStruct((ne, LANES), F32)),
        grid=(n // tt,),
        in_specs=[pl.BlockSpec((tt, d), lambda i: (i, 0)),
                  pl.BlockSpec((ne, d), lambda i: (0, 0)),
                  pl.BlockSpec((ne, 1), lambda i: (0, 0))],
        out_specs=(pl.BlockSpec((TOP_K, tt), lambda i: (0, i)),
                   pl.BlockSpec((TOP_K, tt), lambda i: (0, i)),
                   pl.BlockSpec((TOP_K, tt), lambda i: (0, i)),
                   pl.BlockSpec((ne, LANES), lambda i: (0, 0))),
        compiler_params=_cparams(("arbitrary",)),
        name="router",
    )(x, w_router.T, router_bias.reshape(ne, 1))


def _dest_kernel(idx_ref, rank_ref, pstart_ref, dest_ref):
    ne = pstart_ref.shape[0]
    tt = idx_ref.shape[1]
    iota_e = lax.broadcasted_iota(jnp.int32, (ne, tt), 0)
    pstart = pstart_ref[...]
    idx = idx_ref[...]
    rows = [jnp.sum(jnp.where(iota_e == idx[k:k + 1, :], pstart, 0), axis=0, keepdims=True)
            for k in range(TOP_K)]
    dest_ref[0] = jnp.concatenate(rows, axis=0) + rank_ref[...]


def _dest_rows(eidx_t, rank_t, pstart):
    k, n = eidx_t.shape
    ne = pstart.shape[0]
    tt = DISPATCH_TILE
    return pl.pallas_call(
        _dest_kernel,
        out_shape=jax.ShapeDtypeStruct((n // tt, k, tt), jnp.int32),
        grid=(n // tt,),
        in_specs=[pl.BlockSpec((k, tt), lambda i: (0, i)),
                  pl.BlockSpec((k, tt), lambda i: (0, i)),
                  pl.BlockSpec((ne, 1), lambda i: (0, 0))],
        out_specs=pl.BlockSpec((1, k, tt), lambda i: (i, 0, 0)),
        compiler_params=_cparams(("parallel",)),
        name="dest_rows",
    )(eidx_t, rank_t, pstart.reshape(ne, 1))


def _dispatch_kernel(dest_ref, x_ref, xs_ref, sem):
    tm = x_ref.shape[0]

    def row_copy(t, k):
        return pltpu.make_async_copy(x_ref.at[pl.ds(t, 1)],
                                     xs_ref.at[pl.ds(dest_ref[0, k, t], 1)], sem)

    def start(t, carry):
        for k in range(TOP_K):
            row_copy(t, k).start()
        return carry

    lax.fori_loop(0, tm, start, 0)
    for k in range(TOP_K):
        pltpu.make_async_copy(x_ref, xs_ref.at[pl.ds(0, tm)], sem).wait()


def _dispatch(x, dest3, p_rows):
    n, d = x.shape
    nt, k, tm = dest3.shape
    return pl.pallas_call(
        _dispatch_kernel,
        out_shape=jax.ShapeDtypeStruct((p_rows, d), x.dtype),
        grid=(nt,),
        in_specs=[pl.BlockSpec((1, k, tm), lambda i: (i, 0, 0), memory_space=pltpu.SMEM),
                  pl.BlockSpec((tm, d), lambda i: (i, 0))],
        out_specs=pl.BlockSpec(memory_space=pl.ANY),
        scratch_shapes=[pltpu.SemaphoreType.DMA(())],
        compiler_params=_cparams(("arbitrary",)),
        name="dispatch",
    )(dest3, x)


def _expert_kernel(be_ref, nv_ref, x_ref, wgu_ref, wd_ref, y_ref):
    i = pl.program_id(0)
    f = wd_ref.shape[2]

    @pl.when(i < nv_ref[0])
    def _():
        h = _dot(x_ref[...].astype(BF16), wgu_ref[0, 0].astype(BF16))
        g = h[:, :f]
        act = g * _sigmoid(g) * h[:, f:]
        y_ref[...] = _dot(act.astype(BF16), wd_ref[0, 0].astype(BF16))

    @pl.when(i >= nv_ref[0])
    def _():
        y_ref[...] = jnp.zeros(y_ref.shape, y_ref.dtype)


def _experts(xs, block_e, n_valid_blocks, w_gate_up, w_down, layer):
    p, d = xs.shape
    nblk = p // MOE_BLK
    f2 = w_gate_up.shape[3]
    f = w_down.shape[2]
    grid_spec = pltpu.PrefetchScalarGridSpec(
        num_scalar_prefetch=2,
        grid=(nblk,),
        in_specs=[pl.BlockSpec((MOE_BLK, d), lambda i, be, nv: (jnp.minimum(i, nv[0] - 1), 0)),
                  pl.BlockSpec((1, 1, d, f2), lambda i, be, nv: (layer, be[i], 0, 0)),
                  pl.BlockSpec((1, 1, f, d), lambda i, be, nv: (layer, be[i], 0, 0))],
        out_specs=pl.BlockSpec((MOE_BLK, d), lambda i, be, nv: (i, 0)),
    )
    return pl.pallas_call(
        _expert_kernel,
        out_shape=jax.ShapeDtypeStruct((p, d), F32),
        grid_spec=grid_spec,
        compiler_params=_cparams(("arbitrary",)),
        name="routed_experts",
    )(block_e, n_valid_blocks, xs, w_gate_up, w_down)


def _moe_final_kernel(dest_ref, x_ref, gate_ref, ys_ref, wsg_ref, wsd_ref, g_ref, b_ref,
                      o_ref, ob_ref, ybuf_ref, sem):
    tm = x_ref.shape[0]
    f = wsd_ref.shape[0]

    def row_copy(t, k):
        return pltpu.make_async_copy(ys_ref.at[pl.ds(dest_ref[0, k, t], 1)],
                                     ybuf_ref.at[k, pl.ds(t, 1)], sem)

    def start(t, carry):
        for k in range(TOP_K):
            row_copy(t, k).start()
        return carry

    lax.fori_loop(0, tm, start, 0)
    x = x_ref[...]
    h = _dot(x.astype(BF16), wsg_ref[...])
    g = h[:, :f]
    acc = _dot((g * _sigmoid(g) * h[:, f:]).astype(BF16), wsd_ref[...])
    for k in range(TOP_K):
        pltpu.make_async_copy(ys_ref.at[pl.ds(0, tm)], ybuf_ref.at[k], sem).wait()
    gates = gate_ref[...]
    for k in range(TOP_K):
        acc = acc + ybuf_ref[k] * gates[:, k:k + 1]
    out = _layer_norm(DN_ALPHA * x + acc, g_ref[...], b_ref[...])
    o_ref[...] = out
    ob_ref[...] = out.astype(BF16)


def _moe_final(x, dest3, gates, ys, w_sgu, w_sd, g, b):
    n, d = x.shape
    nt, k, tm = dest3.shape
    return pl.pallas_call(
        _moe_final_kernel,
        out_shape=(jax.ShapeDtypeStruct((n, d), F32), jax.ShapeDtypeStruct((n, d), BF16)),
        grid=(nt,),
        in_specs=[pl.BlockSpec((1, k, tm), lambda i: (i, 0, 0), memory_space=pltpu.SMEM),
                  pl.BlockSpec((tm, d), lambda i: (i, 0)),
                  pl.BlockSpec((tm, k), lambda i: (i, 0)),
                  pl.BlockSpec(memory_space=pl.ANY),
                  pl.BlockSpec(w_sgu.shape, lambda i: (0, 0)),
                  pl.BlockSpec(w_sd.shape, lambda i: (0, 0)),
                  pl.BlockSpec((1, d), lambda i: (0, 0)),
                  pl.BlockSpec((1, d), lambda i: (0, 0))],
        out_specs=(pl.BlockSpec((tm, d), lambda i: (i, 0)),
                   pl.BlockSpec((tm, d), lambda i: (i, 0))),
        scratch_shapes=[pltpu.VMEM((k, tm, d), ys.dtype), pltpu.SemaphoreType.DMA(())],
        compiler_params=_cparams(("arbitrary",)),
        name="moe_final",
    )(dest3, x, gates, ys, w_sgu, w_sd, g.reshape(1, d), b.reshape(1, d))


def _block_plan(counts, n_assign):
    ne = counts.shape[0]
    nblk = (n_assign + ne * MOE_BLK) // MOE_BLK
    padded = (counts + MOE_BLK - 1) // MOE_BLK * MOE_BLK
    pend = jnp.cumsum(padded)
    pstart = pend - padded
    n_valid = pend[-1] // MOE_BLK
    blk_start = jnp.arange(nblk, dtype=jnp.int32) * MOE_BLK
    block_e = jnp.minimum(jnp.searchsorted(pend, blk_start, side='right'), ne - 1).astype(jnp.int32)
    last_e = block_e[jnp.maximum(n_valid - 1, 0)]
    block_e = jnp.where(jnp.arange(nblk) < n_valid, block_e, last_e)
    return pstart.astype(jnp.int32), block_e, n_valid.astype(jnp.int32).reshape(1), nblk * MOE_BLK


def _moe_layer(x_f32, layer, w_router, router_bias, w_gate_up, w_down, w_sgu, w_sd, ln_g, ln_b):
    eidx_t, gates_t, rank_t, cnt = _router(x_f32, w_router, router_bias)
    counts = cnt[:, 0].astype(jnp.int32)
    pstart, block_e, n_valid, p_rows = _block_plan(counts, eidx_t.shape[0] * eidx_t.shape[1])
    dest3 = _dest_rows(eidx_t, rank_t, pstart)
    xs = _dispatch(x_f32, dest3, p_rows)
    ys = _experts(xs, block_e, n_valid, w_gate_up, w_down, layer)
    return _moe_final(x_f32, dest3, gates_t.T, ys, w_sgu.astype(BF16), w_sd.astype(BF16), ln_g, ln_b)


def kernel(x, a_w_in, a_conv_w, a_a_log, a_dt_bias, a_out_norm_g, a_w_out, w_kv_shared, b_w_q, b_rel_bias, b_w_out, moe_w_router, moe_router_bias, moe_w_gate_up, moe_w_down, moe_w_shared_gate_up, moe_w_shared_down, ln_mix_g, ln_mix_b, ln_ffn_g, ln_ffn_b):
    bsz, seq, d = x.shape
    n = bsz * seq
    xf = x.reshape(n, d)
    xb = xf.astype(BF16)

    main_w = 4 * DN_HEADS * DN_DK
    w_in = a_w_in[0]
    proj = _matmul(xb, w_in[:, :main_w].astype(BF16), F32, 512, 1024)
    w_ab = jnp.zeros((d, LANES), F32).at[:, :2 * DN_HEADS].set(w_in[:, main_w:]).astype(BF16)
    ab = _matmul(xb, w_ab, F32, 512, LANES)
    o = _deltanet(proj, ab, a_conv_w[0], a_a_log[0], a_dt_bias[0], a_out_norm_g[0], bsz, seq)
    x1, _ = _matmul_res_ln(o, a_w_out[0].astype(BF16), xf, ln_mix_g[0], ln_mix_b[0])
    x2, x2b = _moe_layer(x1, 0, moe_w_router[0], moe_router_bias[0], moe_w_gate_up, moe_w_down,
                         moe_w_shared_gate_up[0], moe_w_shared_down[0], ln_ffn_g[0], ln_ffn_b[0])

    kv = _matmul(x2b, w_kv_shared.astype(BF16), BF16, 512, 1024)
    q = _matmul(x2b, b_w_q[0].astype(BF16), BF16, 512, 1024)
    att = _attention(q, kv, _rel_bias(b_rel_bias[0]), bsz, seq)
    x3, _ = _matmul_res_ln(att, b_w_out[0].astype(BF16), x2, ln_mix_g[1], ln_mix_b[1])
    x4, _ = _moe_layer(x3, 1, moe_w_router[1], moe_router_bias[1], moe_w_gate_up, moe_w_down,
                       moe_w_shared_gate_up[1], moe_w_shared_down[1], ln_ffn_g[1], ln_ffn_b[1])
    return x4.reshape(bsz, seq, d)
```

```python
import jax
import jax.numpy as jnp
from jax import lax
from jax.experimental import pallas as pl
from jax.experimental.pallas import tpu as pltpu

F32 = jnp.float32
BF16 = jnp.bfloat16
HIGHEST = lax.Precision.HIGHEST

CHUNK = 64
DN_HEADS = 8
DN_DK = 128
CONV_WIDTH = 4
ATT_HEADS = 16
ATT_DH = 64
LEFT_CHUNKS = 8
MAX_REL = 256
N_EXPERTS = 256
TOP_K = 8
N_GROUPS = 8
TOPK_GROUPS = 4
ROUTED_SCALE = 2.5
DEPTH = 2
DN_ALPHA = (2 * DEPTH) ** 0.25
LN_EPS = 1e-5
NORM_EPS = 1e-6

LANES = 128
VMEM_LIMIT = 56 * 1024 * 1024

DN_TILE = 2 * CHUNK
ATT_TILE = 8 * CHUNK
ATT_BAND = (LEFT_CHUNKS + 2) * CHUNK
MOE_BLK = 256
DISPATCH_TILE = 256
DEST_TOK = LANES // TOP_K
DMA_UNROLL_TOK = 4
ROUTER_TILE = 512


def _cparams(sem):
    return pltpu.CompilerParams(dimension_semantics=sem, vmem_limit_bytes=VMEM_LIMIT)


def _sigmoid(x):
    return 1.0 / (1.0 + jnp.exp(-x))


def _dot(a, b):
    return jnp.dot(a, b, preferred_element_type=F32)


def _dot_nt(a, b, precision=None):
    return lax.dot_general(a, b, (((1,), (1,)), ((), ())), precision=precision,
                           preferred_element_type=F32)


def _dot_tn(a, b):
    return lax.dot_general(a, b, (((0,), (0,)), ((), ())), preferred_element_type=F32)


def _layer_norm(y, g, b):
    mu = jnp.mean(y, axis=-1, keepdims=True)
    d = y - mu
    var = jnp.mean(d * d, axis=-1, keepdims=True)
    return d * lax.rsqrt(var + LN_EPS) * g + b


def _mm_kernel(a_ref, w_ref, o_ref):
    o_ref[...] = _dot(a_ref[...], w_ref[...]).astype(o_ref.dtype)


def _matmul(a, w, out_dtype, tm, tn):
    m, k = a.shape
    n = w.shape[1]
    return pl.pallas_call(
        _mm_kernel,
        out_shape=jax.ShapeDtypeStruct((m, n), out_dtype),
        grid=(m // tm, n // tn),
        in_specs=[pl.BlockSpec((tm, k), lambda i, j: (i, 0)),
                  pl.BlockSpec((k, tn), lambda i, j: (0, j))],
        out_specs=pl.BlockSpec((tm, tn), lambda i, j: (i, j)),
        compiler_params=_cparams(("parallel", "arbitrary")),
        name="matmul",
    )(a, w)


def _mm_res_ln_kernel(a_ref, w_ref, x_ref, g_ref, b_ref, o_ref, ob_ref):
    y = DN_ALPHA * x_ref[...] + _dot(a_ref[...], w_ref[...])
    out = _layer_norm(y, g_ref[...], b_ref[...])
    o_ref[...] = out
    ob_ref[...] = out.astype(BF16)


def _matmul_res_ln(a, w, x, g, b, tm=512):
    m, k = a.shape
    d = w.shape[1]
    return pl.pallas_call(
        _mm_res_ln_kernel,
        out_shape=(jax.ShapeDtypeStruct((m, d), F32), jax.ShapeDtypeStruct((m, d), BF16)),
        grid=(m // tm,),
        in_specs=[pl.BlockSpec((tm, k), lambda i: (i, 0)),
                  pl.BlockSpec((k, d), lambda i: (0, 0)),
                  pl.BlockSpec((tm, d), lambda i: (i, 0)),
                  pl.BlockSpec((1, d), lambda i: (0, 0)),
                  pl.BlockSpec((1, d), lambda i: (0, 0))],
        out_specs=(pl.BlockSpec((tm, d), lambda i: (i, 0)),
                   pl.BlockSpec((tm, d), lambda i: (i, 0))),
        compiler_params=_cparams(("parallel",)),
        name="matmul_res_ln",
    )(a, w, x, g.reshape(1, d), b.reshape(1, d))


def _dn_kernel(qkv_ref, z_ref, ab_ref, cw_ref, alog_ref, dtb_ref, ong_ref, o_ref,
               xc_ref, s_ref):
    t = DN_TILE
    dk = DN_DK
    qk_w = DN_HEADS * dk
    c_idx = pl.program_id(1)

    @pl.when(c_idx == 0)
    def _():
        xc_ref[0:8, :] = jnp.zeros((8, xc_ref.shape[1]), F32)
        s_ref[...] = jnp.zeros(s_ref.shape, F32)

    xc_ref[8:8 + t, :] = qkv_ref[...]

    ab = ab_ref[...]
    a_sh = ab + dtb_ref[...]
    softplus = jnp.maximum(a_sh, 0.0) + jnp.log(1.0 + jnp.exp(-jnp.abs(a_sh)))
    g_full = -jnp.exp(alog_ref[...]) * softplus
    beta_full = _sigmoid(ab)

    row = lax.broadcasted_iota(jnp.int32, (t, t), 0)
    col = lax.broadcasted_iota(jnp.int32, (t, t), 1)
    same = (row // CHUNK) == (col // CHUNK)
    m_incl = same & (row >= col)
    m_strict = same & (row > col)
    l_incl = jnp.where(m_incl, 1.0, 0.0).astype(F32)
    l_all = jnp.where(same, 1.0, 0.0).astype(F32)
    gc_all = jnp.dot(l_incl, g_full, precision=HIGHEST, preferred_element_type=F32)
    gl_all = jnp.dot(l_all, g_full, precision=HIGHEST, preferred_element_type=F32)
    gc_t = gc_all.T

    def conv_act(off):
        acc = xc_ref[5:5 + t, off:off + dk] * cw_ref[0:1, off:off + dk]
        for j in range(1, CONV_WIDTH):
            acc = acc + xc_ref[5 + j:5 + j + t, off:off + dk] * cw_ref[j:j + 1, off:off + dk]
        return acc * _sigmoid(acc)

    hs = range(DN_HEADS)
    q, k, kb_, egc, gls, gcs = [], [], [], [], [], []
    a_b, p_mat, y = [], [], []
    for h in hs:
        qh = conv_act(h * dk)
        kh = conv_act(qk_w + h * dk)
        vh = conv_act(2 * qk_w + h * dk)
        qh = qh * lax.rsqrt(jnp.sum(qh * qh, axis=-1, keepdims=True) + NORM_EPS) * (dk ** -0.5)
        kh = kh * lax.rsqrt(jnp.sum(kh * kh, axis=-1, keepdims=True) + NORM_EPS)
        beta = beta_full[:, DN_HEADS + h:DN_HEADS + h + 1]
        gc = gc_all[:, h:h + 1]
        gr = gc_t[h:h + 1, :]
        decay = jnp.exp(jnp.where(m_incl, gc - gr, -jnp.inf))
        kb = kh * beta
        k_b = kh.astype(BF16)
        a_b.append(jnp.where(m_strict, _dot_nt(kb.astype(BF16), k_b) * decay, 0.0).astype(BF16))
        p_mat.append((_dot_nt(qh.astype(BF16), k_b) * decay).astype(BF16))
        e = jnp.exp(gc)
        y.append(jnp.concatenate([vh * beta, kb * e], axis=1))
        q.append(qh)
        k.append(kh)
        egc.append(e)
        gcs.append(gc)
        gls.append(gl_all[:, h:h + 1])

    y = [y[h] - _dot(a_b[h], y[h].astype(BF16)) for h in hs]
    for _ in range(5):
        a_b = [_dot(a_b[h], a_b[h]).astype(BF16) for h in hs]
        y = [y[h] + _dot(a_b[h], y[h].astype(BF16)) for h in hs]

    wq0, wq1, kd0, kd1 = [], [], [], []
    for h in hs:
        w = y[h][:, dk:]
        qg = q[h] * egc[h]
        kd = (k[h] * jnp.exp(gls[h] - gcs[h])).astype(BF16)
        wq0.append(jnp.concatenate([w[:CHUNK], qg[:CHUNK]], axis=0).astype(BF16))
        wq1.append(jnp.concatenate([w[CHUNK:], qg[CHUNK:]], axis=0).astype(BF16))
        kd0.append(kd[:CHUNK])
        kd1.append(kd[CHUNK:])

    s0 = [s_ref[h] for h in hs]
    r0 = [_dot(wq0[h], s0[h].astype(BF16)) for h in hs]
    vn0 = [y[h][:CHUNK, :dk] - r0[h][:CHUNK] for h in hs]
    s1 = [s0[h] * jnp.exp(gls[h][0:1, :]) + _dot_tn(kd0[h], vn0[h].astype(BF16)) for h in hs]
    r1 = [_dot(wq1[h], s1[h].astype(BF16)) for h in hs]
    vn1 = [y[h][CHUNK:, :dk] - r1[h][:CHUNK] for h in hs]
    for h in hs:
        s_ref[h] = (s1[h] * jnp.exp(gls[h][CHUNK:CHUNK + 1, :])
                    + _dot_tn(kd1[h], vn1[h].astype(BF16)))
    for h in hs:
        vn = jnp.concatenate([vn0[h], vn1[h]], axis=0).astype(BF16)
        o = jnp.concatenate([r0[h][CHUNK:], r1[h][CHUNK:]], axis=0) + _dot(p_mat[h], vn)
        o = o * lax.rsqrt(jnp.mean(o * o, axis=-1, keepdims=True) + NORM_EPS) * ong_ref[...]
        zh = z_ref[:, h * dk:(h + 1) * dk]
        o = o * (zh * _sigmoid(zh))
        o_ref[:, h * dk:(h + 1) * dk] = o.astype(o_ref.dtype)

    xc_ref[0:8, :] = xc_ref[t:t + 8, :]


def _deltanet(proj, ab, conv_w, a_log, dt_bias, out_norm_g, bsz, seq):
    n = proj.shape[0]
    t = DN_TILE
    nct = seq // t
    qkv_w = 3 * DN_HEADS * DN_DK
    v_w = DN_HEADS * DN_DK
    alog_p = jnp.zeros((1, LANES), F32).at[0, :DN_HEADS].set(a_log)
    dtb_p = jnp.zeros((1, LANES), F32).at[0, :DN_HEADS].set(dt_bias)
    return pl.pallas_call(
        _dn_kernel,
        out_shape=jax.ShapeDtypeStruct((n, v_w), BF16),
        grid=(bsz, nct),
        in_specs=[pl.BlockSpec((t, qkv_w), lambda b, c: (b * nct + c, 0)),
                  pl.BlockSpec((t, v_w), lambda b, c: (b * nct + c, qkv_w // v_w)),
                  pl.BlockSpec((t, LANES), lambda b, c: (b * nct + c, 0)),
                  pl.BlockSpec((CONV_WIDTH, qkv_w), lambda b, c: (0, 0)),
                  pl.BlockSpec((1, LANES), lambda b, c: (0, 0)),
                  pl.BlockSpec((1, LANES), lambda b, c: (0, 0)),
                  pl.BlockSpec((1, DN_DK), lambda b, c: (0, 0))],
        out_specs=pl.BlockSpec((t, v_w), lambda b, c: (b * nct + c, 0)),
        scratch_shapes=[pltpu.VMEM((t + 8, qkv_w), F32),
                        pltpu.VMEM((DN_HEADS, DN_DK, DN_DK), F32)],
        compiler_params=_cparams(("arbitrary", "arbitrary")),
        name="deltanet",
    )(proj, proj, ab, conv_w, alog_p, dtb_p, out_norm_g.reshape(1, DN_DK))


def _att_kernel(q_ref, kp_ref, kc_ref, vp_ref, vc_ref, bias_ref, o_ref, ks_ref, vs_ref):
    tq = ATT_TILE
    t_idx = pl.program_id(1)
    d = q_ref.shape[1]
    zeros = jnp.zeros((CHUNK, d), BF16)
    ks_ref[0:CHUNK, :] = zeros
    vs_ref[0:CHUNK, :] = zeros
    ks_ref[CHUNK:CHUNK + tq, :] = kp_ref[...]
    vs_ref[CHUNK:CHUNK + tq, :] = vp_ref[...]
    ks_ref[CHUNK + tq:CHUNK + 2 * tq, :] = kc_ref[...]
    vs_ref[CHUNK + tq:CHUNK + 2 * tq, :] = vc_ref[...]

    lane = lax.broadcasted_iota(jnp.int32, (CHUNK, 2 * ATT_DH), 1)
    first = lane < ATT_DH
    j_band = lax.broadcasted_iota(jnp.int32, (2 * CHUNK, ATT_BAND), 1)

    def chunk_body(c, carry):
        r0 = pl.multiple_of(c * CHUNK, CHUNK)
        kpos = t_idx * tq - (LEFT_CHUNKS + 1) * CHUNK + c * CHUNK + j_band
        valid = (j_band >= CHUNK) & (kpos >= 0)
        pairs = range(ATT_HEADS // 2)
        lanes = [slice(hp * 2 * ATT_DH, (hp + 1) * 2 * ATT_DH) for hp in pairs]
        scores = []
        for hp in pairs:
            qp = q_ref[pl.ds(r0, CHUNK), lanes[hp]] * jnp.asarray(ATT_DH ** -0.5, BF16)
            zq = jnp.zeros_like(qp)
            q2 = jnp.concatenate([jnp.where(first, qp, zq), jnp.where(first, zq, qp)], axis=0)
            s = _dot_nt(q2, ks_ref[pl.ds(r0, ATT_BAND), lanes[hp]])
            s = s + jnp.concatenate([bias_ref[2 * hp], bias_ref[2 * hp + 1]], axis=0)
            scores.append(jnp.where(valid, s, -jnp.inf))
        probs, denom = [], []
        for hp in pairs:
            m = jnp.max(scores[hp], axis=-1, keepdims=True)
            p = jnp.exp(scores[hp] - m)
            denom.append(jnp.sum(p, axis=-1, keepdims=True))
            probs.append(p.astype(BF16))
        for hp in pairs:
            r = _dot(probs[hp], vs_ref[pl.ds(r0, ATT_BAND), lanes[hp]]) / denom[hp]
            o_ref[pl.ds(r0, CHUNK), lanes[hp]] = jnp.where(first, r[:CHUNK], r[CHUNK:]).astype(o_ref.dtype)
        return carry

    lax.fori_loop(0, tq // CHUNK, chunk_body, 0)


def _attention(q, kv, bias, bsz, seq):
    n, d = q.shape
    tq = ATT_TILE
    nt = seq // tq
    rows = CHUNK + 2 * tq
    prev = lambda b, t: (b * nt + jnp.maximum(t - 1, 0), 0)
    cur = lambda b, t: (b * nt + t, 0)
    prev_v = lambda b, t: (b * nt + jnp.maximum(t - 1, 0), 1)
    cur_v = lambda b, t: (b * nt + t, 1)
    return pl.pallas_call(
        _att_kernel,
        out_shape=jax.ShapeDtypeStruct((n, d), BF16),
        grid=(bsz, nt),
        in_specs=[pl.BlockSpec((tq, d), cur),
                  pl.BlockSpec((tq, d), prev),
                  pl.BlockSpec((tq, d), cur),
                  pl.BlockSpec((tq, d), prev_v),
                  pl.BlockSpec((tq, d), cur_v),
                  pl.BlockSpec((ATT_HEADS, CHUNK, ATT_BAND), lambda b, t: (0, 0, 0))],
        out_specs=pl.BlockSpec((tq, d), cur),
        scratch_shapes=[pltpu.VMEM((rows, d), BF16), pltpu.VMEM((rows, d), BF16)],
        compiler_params=_cparams(("parallel", "arbitrary")),
        name="band_attention",
    )(q, kv, kv, kv, kv, bias)


def _rel_bias(table):
    i_loc = jnp.arange(CHUNK)
    j_band = jnp.arange(ATT_BAND)
    rel = i_loc[:, None] + (LEFT_CHUNKS + 1) * CHUNK - j_band[None, :]
    rel_idx = jnp.clip(rel, -(CHUNK - 1), MAX_REL) + (CHUNK - 1)
    return table[:, rel_idx].astype(F32)


def _router_kernel(x_ref, wt_ref, bias_ref, idx_ref, gate_ref, rank_ref, cnt_ref):
    tt = x_ref.shape[0]
    ne = wt_ref.shape[0]
    gsz = ne // N_GROUPS

    @pl.when(pl.program_id(0) == 0)
    def _():
        cnt_ref[...] = jnp.zeros(cnt_ref.shape, F32)

    logits = _dot_nt(wt_ref[...], x_ref[...], precision=HIGHEST)
    scores = _sigmoid(logits)
    sel = scores + bias_ref[...]
    neg = -jnp.inf

    iota_g = lax.broadcasted_iota(jnp.int32, (gsz, tt), 0)
    gs_rows = []
    for g in range(N_GROUPS):
        blk = sel[g * gsz:(g + 1) * gsz, :]
        m1 = jnp.max(blk, axis=0, keepdims=True)
        i1 = jnp.min(jnp.where(blk == m1, iota_g, gsz), axis=0, keepdims=True)
        m2 = jnp.max(jnp.where(iota_g == i1, neg, blk), axis=0, keepdims=True)
        gs_rows.append(m1 + m2)
    gsc = jnp.concatenate(gs_rows, axis=0)
    iota_n = lax.broadcasted_iota(jnp.int32, (N_GROUPS, tt), 0)
    chosen = jnp.zeros((N_GROUPS, tt), F32)
    for _ in range(TOPK_GROUPS):
        m = jnp.max(gsc, axis=0, keepdims=True)
        i = jnp.min(jnp.where(gsc == m, iota_n, N_GROUPS), axis=0, keepdims=True)
        hit = iota_n == i
        chosen = jnp.where(hit, 1.0, chosen)
        gsc = jnp.where(hit, neg, gsc)
    selm = jnp.concatenate(
        [jnp.where(chosen[g:g + 1, :] > 0.0, sel[g * gsz:(g + 1) * gsz, :], neg)
         for g in range(N_GROUPS)], axis=0)

    iota_e = lax.broadcasted_iota(jnp.int32, (ne, tt), 0)
    idx_rows, w_rows, hits = [], [], []
    for _ in range(TOP_K):
        m = jnp.max(selm, axis=0, keepdims=True)
        i = jnp.min(jnp.where(selm == m, iota_e, ne), axis=0, keepdims=True)
        hit = iota_e == i
        w_rows.append(jnp.sum(jnp.where(hit, scores, 0.0), axis=0, keepdims=True))
        idx_rows.append(i)
        hits.append(hit)
        selm = jnp.where(hit, neg, selm)
    wts = jnp.concatenate(w_rows, axis=0)
    wts = wts / jnp.sum(wts, axis=0, keepdims=True) * ROUTED_SCALE
    idx_ref[...] = jnp.concatenate(idx_rows, axis=0)
    gate_ref[...] = wts

    onehot = jnp.where(hits[0], 1.0, 0.0)
    for hit in hits[1:]:
        onehot = onehot + jnp.where(hit, 1.0, 0.0)
    onehot = onehot.astype(BF16)
    t_row = lax.broadcasted_iota(jnp.int32, (tt, tt), 0)
    t_col = lax.broadcasted_iota(jnp.int32, (tt, tt), 1)
    before = jnp.where(t_row < t_col, 1.0, 0.0).astype(BF16)
    pos = cnt_ref[:, 0:1] + _dot(onehot, before)
    rank_rows = [jnp.sum(jnp.where(hit, pos, 0.0), axis=0, keepdims=True) for hit in hits]
    rank_ref[...] = jnp.concatenate(rank_rows, axis=0).astype(jnp.int32)
    cnt_ref[...] = cnt_ref[...] + _dot(onehot, jnp.ones((tt, LANES), BF16))


def _router(x, w_router, router_bias):
    n, d = x.shape
    ne = w_router.shape[1]
    tt = ROUTER_TILE
    return pl.pallas_call(
        _router_kernel,
        out_shape=(jax.ShapeDtypeStruct((TOP_K, n), jnp.int32),
                   jax.ShapeDtypeStruct((TOP_K, n), F32),
                   jax.ShapeDtypeStruct((TOP_K, n), jnp.int32),
                   jax.ShapeDtypeStruct((ne, LANES), F32)),
        grid=(n // tt,),
        in_specs=[pl.BlockSpec((tt, d), lambda i: (i, 0)),
                  pl.BlockSpec((ne, d), lambda i: (0, 0)),
                  pl.BlockSpec((ne, 1), lambda i: (0, 0))],
        out_specs=(pl.BlockSpec((TOP_K, tt), lambda i: (0, i)),
                   pl.BlockSpec((TOP_K, tt), lambda i: (0, i)),
                   pl.BlockSpec((TOP_K, tt), lambda i: (0, i)),
                   pl.BlockSpec((ne, LANES), lambda i: (0, 0))),
        compiler_params=_cparams(("arbitrary",)),
        name="router",
    )(x, w_router.T, router_bias.reshape(ne, 1))


def _dest_kernel(idx_ref, rank_ref, pstart_ref, dest_ref):
    ne = pstart_ref.shape[0]
    tt = idx_ref.shape[1]
    iota_e = lax.broadcasted_iota(jnp.int32, (ne, tt), 0)
    pstart = pstart_ref[...]
    idx = idx_ref[...]
    rows = [jnp.sum(jnp.where(iota_e == idx[k:k + 1, :], pstart, 0), axis=0, keepdims=True)
            for k in range(TOP_K)]
    dest_ref[...] = jnp.concatenate(rows, axis=0) + rank_ref[...]


def _dest_rows(eidx_t, rank_t, pstart):
    k, n = eidx_t.shape
    ne = pstart.shape[0]
    tt = ROUTER_TILE
    dest_t = pl.pallas_call(
        _dest_kernel,
        out_shape=jax.ShapeDtypeStruct((k, n), jnp.int32),
        grid=(n // tt,),
        in_specs=[pl.BlockSpec((k, tt), lambda i: (0, i)),
                  pl.BlockSpec((k, tt), lambda i: (0, i)),
                  pl.BlockSpec((ne, 1), lambda i: (0, 0))],
        out_specs=pl.BlockSpec((k, tt), lambda i: (0, i)),
        compiler_params=_cparams(("parallel",)),
        name="dest_rows",
    )(eidx_t, rank_t, pstart.reshape(ne, 1))
    tiled = dest_t.reshape(k, n // DEST_TOK, DEST_TOK).transpose(1, 2, 0)
    return tiled.reshape(n // DISPATCH_TILE, DISPATCH_TILE // DEST_TOK, DEST_TOK * k)


def _for_each_row(tm, fn):
    for jg in range(DEST_TOK // DMA_UNROLL_TOK):
        def body(r, carry, jg=jg):
            t0 = pl.multiple_of(r * DEST_TOK, DEST_TOK)
            for j in range(jg * DMA_UNROLL_TOK, (jg + 1) * DMA_UNROLL_TOK):
                for k in range(TOP_K):
                    fn(t0 + j, k, (r, j * TOP_K + k))
            return carry

        lax.fori_loop(0, tm // DEST_TOK, body, 0)


def _dispatch_kernel(pend_ref, nv_ref, dest_ref, x_ref, xs_ref, zero_ref, sem, zsem):
    tm = x_ref.shape[0]
    ne = pend_ref.shape[0]
    nblk = xs_ref.shape[0] // MOE_BLK

    @pl.when(pl.program_id(0) == 0)
    def _():
        zero_ref[...] = jnp.zeros(zero_ref.shape, zero_ref.dtype)

        def zero_copy(row0):
            return pltpu.make_async_copy(
                zero_ref, xs_ref.at[pl.ds(pl.multiple_of(row0, MOE_BLK), MOE_BLK)], zsem)

        def has_rows(e):
            return pend_ref[e] > jnp.where(e > 0, pend_ref[jnp.maximum(e - 1, 0)], 0)

        def each_block(fn):
            def per_expert(e, carry):
                @pl.when(has_rows(e))
                def _():
                    fn(zero_copy(pend_ref[e] - MOE_BLK))
                return carry

            def per_tail(b, carry):
                fn(zero_copy(b * MOE_BLK))
                return carry

            lax.fori_loop(0, ne, per_expert, 0)
            lax.fori_loop(nv_ref[0], nblk, per_tail, 0)

        each_block(lambda cp: cp.start())
        each_block(lambda cp: cp.wait())

    def start_row(t, k, entry):
        pltpu.make_async_copy(x_ref.at[pl.ds(t, 1)],
                              xs_ref.at[pl.ds(dest_ref[0, entry[0], entry[1]], 1)], sem).start()

    _for_each_row(tm, start_row)
    for k in range(TOP_K):
        pltpu.make_async_copy(x_ref, xs_ref.at[pl.ds(0, tm)], sem).wait()


def _dispatch(x, dest3, pend, n_valid_blocks, p_rows):
    n, d = x.shape
    nt = dest3.shape[0]
    tm = DISPATCH_TILE
    grid_spec = pltpu.PrefetchScalarGridSpec(
        num_scalar_prefetch=2,
        grid=(nt,),
        in_specs=[pl.BlockSpec((1,) + dest3.shape[1:], lambda i, pe, nv: (i, 0, 0),
                               memory_space=pltpu.SMEM),
                  pl.BlockSpec((tm, d), lambda i, pe, nv: (i, 0))],
        out_specs=pl.BlockSpec(memory_space=pl.ANY),
        scratch_shapes=[pltpu.VMEM((MOE_BLK, d), x.dtype),
                        pltpu.SemaphoreType.DMA(()), pltpu.SemaphoreType.DMA(())],
    )
    return pl.pallas_call(
        _dispatch_kernel,
        out_shape=jax.ShapeDtypeStruct((p_rows, d), x.dtype),
        grid_spec=grid_spec,
        compiler_params=_cparams(("arbitrary",)),
        name="dispatch",
    )(pend, n_valid_blocks, dest3, x)


def _expert_kernel(be_ref, nv_ref, x_ref, wgu_ref, wd_ref, y_ref):
    i = pl.program_id(0)
    f = wd_ref.shape[2]

    @pl.when(i < nv_ref[0])
    def _():
        h = _dot(x_ref[...].astype(BF16), wgu_ref[0, 0].astype(BF16))
        g = h[:, :f]
        act = g * _sigmoid(g) * h[:, f:]
        y_ref[...] = _dot(act.astype(BF16), wd_ref[0, 0].astype(BF16))

    @pl.when(i >= nv_ref[0])
    def _():
        y_ref[...] = jnp.zeros(y_ref.shape, y_ref.dtype)


def _experts(xs, block_e, n_valid_blocks, w_gate_up, w_down, layer):
    p, d = xs.shape
    nblk = p // MOE_BLK
    f2 = w_gate_up.shape[3]
    f = w_down.shape[2]
    grid_spec = pltpu.PrefetchScalarGridSpec(
        num_scalar_prefetch=2,
        grid=(nblk,),
        in_specs=[pl.BlockSpec((MOE_BLK, d), lambda i, be, nv: (jnp.minimum(i, nv[0] - 1), 0)),
                  pl.BlockSpec((1, 1, d, f2), lambda i, be, nv: (layer, be[i], 0, 0)),
                  pl.BlockSpec((1, 1, f, d), lambda i, be, nv: (layer, be[i], 0, 0))],
        out_specs=pl.BlockSpec((MOE_BLK, d), lambda i, be, nv: (i, 0)),
    )
    return pl.pallas_call(
        _expert_kernel,
        out_shape=jax.ShapeDtypeStruct((p, d), F32),
        grid_spec=grid_spec,
        compiler_params=_cparams(("arbitrary",)),
        name="routed_experts",
    )(block_e, n_valid_blocks, xs, w_gate_up, w_down)


def _moe_final_kernel(dest_ref, x_ref, gate_ref, ys_ref, wsg_ref, wsd_ref, g_ref, b_ref,
                      o_ref, ob_ref, ybuf_ref, sem):
    tm = x_ref.shape[0]
    f = wsd_ref.shape[0]

    def start_row(t, k, entry):
        pltpu.make_async_copy(ys_ref.at[pl.ds(dest_ref[0, entry[0], entry[1]], 1)],
                              ybuf_ref.at[k, pl.ds(t, 1)], sem).start()

    _for_each_row(tm, start_row)
    x = x_ref[...]
    h = _dot(x.astype(BF16), wsg_ref[...])
    g = h[:, :f]
    acc = _dot((g * _sigmoid(g) * h[:, f:]).astype(BF16), wsd_ref[...])
    for k in range(TOP_K):
        pltpu.make_async_copy(ys_ref.at[pl.ds(0, tm)], ybuf_ref.at[k], sem).wait()
    gates = gate_ref[...]
    for k in range(TOP_K):
        acc = acc + ybuf_ref[k] * gates[:, k:k + 1]
    out = _layer_norm(DN_ALPHA * x + acc, g_ref[...], b_ref[...])
    o_ref[...] = out
    ob_ref[...] = out.astype(BF16)


def _moe_final(x, dest3, gates, ys, w_sgu, w_sd, g, b):
    n, d = x.shape
    nt = dest3.shape[0]
    tm = DISPATCH_TILE
    k = TOP_K
    return pl.pallas_call(
        _moe_final_kernel,
        out_shape=(jax.ShapeDtypeStruct((n, d), F32), jax.ShapeDtypeStruct((n, d), BF16)),
        grid=(nt,),
        in_specs=[pl.BlockSpec((1,) + dest3.shape[1:], lambda i: (i, 0, 0), memory_space=pltpu.SMEM),
                  pl.BlockSpec((tm, d), lambda i: (i, 0)),
                  pl.BlockSpec((tm, k), lambda i: (i, 0)),
                  pl.BlockSpec(memory_space=pl.ANY),
                  pl.BlockSpec(w_sgu.shape, lambda i: (0, 0)),
                  pl.BlockSpec(w_sd.shape, lambda i: (0, 0)),
                  pl.BlockSpec((1, d), lambda i: (0, 0)),
                  pl.BlockSpec((1, d), lambda i: (0, 0))],
        out_specs=(pl.BlockSpec((tm, d), lambda i: (i, 0)),
                   pl.BlockSpec((tm, d), lambda i: (i, 0))),
        scratch_shapes=[pltpu.VMEM((k, tm, d), ys.dtype), pltpu.SemaphoreType.DMA(())],
        compiler_params=_cparams(("arbitrary",)),
        name="moe_final",
    )(dest3, x, gates, ys, w_sgu, w_sd, g.reshape(1, d), b.reshape(1, d))


def _block_plan(counts, n_assign):
    ne = counts.shape[0]
    nblk = (n_assign + ne * MOE_BLK) // MOE_BLK
    padded = (counts + MOE_BLK - 1) // MOE_BLK * MOE_BLK
    pend = jnp.cumsum(padded)
    pstart = pend - padded
    n_valid = pend[-1] // MOE_BLK
    blk_start = jnp.minimum(jnp.arange(nblk, dtype=jnp.int32) * MOE_BLK, pend[-1] - 1)
    block_e = jnp.sum((pend[None, :] <= blk_start[:, None]).astype(jnp.int32), axis=1)
    block_e = jnp.minimum(block_e, ne - 1)
    return (pstart.astype(jnp.int32), pend.astype(jnp.int32), block_e,
            n_valid.astype(jnp.int32).reshape(1), nblk * MOE_BLK)


def _moe_layer(x_f32, layer, w_router, router_bias, w_gate_up, w_down, w_sgu, w_sd, ln_g, ln_b):
    eidx_t, gates_t, rank_t, cnt = _router(x_f32, w_router, router_bias)
    counts = cnt[:, 0].astype(jnp.int32)
    pstart, pend, block_e, n_valid, p_rows = _block_plan(counts, eidx_t.shape[0] * eidx_t.shape[1])
    dest3 = _dest_rows(eidx_t, rank_t, pstart)
    xs = _dispatch(x_f32, dest3, pend, n_valid, p_rows)
    ys = _experts(xs, block_e, n_valid, w_gate_up, w_down, layer)
    return _moe_final(x_f32, dest3, gates_t.T, ys, w_sgu.astype(BF16), w_sd.astype(BF16), ln_g, ln_b)


def kernel(x, a_w_in, a_conv_w, a_a_log, a_dt_bias, a_out_norm_g, a_w_out, w_kv_shared, b_w_q, b_rel_bias, b_w_out, moe_w_router, moe_router_bias, moe_w_gate_up, moe_w_down, moe_w_shared_gate_up, moe_w_shared_down, ln_mix_g, ln_mix_b, ln_ffn_g, ln_ffn_b):
    bsz, seq, d = x.shape
    n = bsz * seq
    xf = x.reshape(n, d)
    xb = xf.astype(BF16)

    main_w = 4 * DN_HEADS * DN_DK
    w_in = a_w_in[0]
    proj = _matmul(xb, w_in[:, :main_w].astype(BF16), F32, 512, 1024)
    w_ab = jnp.zeros((d, LANES), F32).at[:, :2 * DN_HEADS].set(w_in[:, main_w:]).astype(BF16)
    ab = _matmul(xb, w_ab, F32, 512, LANES)
    o = _deltanet(proj, ab, a_conv_w[0], a_a_log[0], a_dt_bias[0], a_out_norm_g[0], bsz, seq)
    x1, _ = _matmul_res_ln(o, a_w_out[0].astype(BF16), xf, ln_mix_g[0], ln_mix_b[0])
    x2, x2b = _moe_layer(x1, 0, moe_w_router[0], moe_router_bias[0], moe_w_gate_up, moe_w_down,
                         moe_w_shared_gate_up[0], moe_w_shared_down[0], ln_ffn_g[0], ln_ffn_b[0])

    kv = _matmul(x2b, w_kv_shared.astype(BF16), BF16, 512, 1024)
    q = _matmul(x2b, b_w_q[0].astype(BF16), BF16, 512, 1024)
    att = _attention(q, kv, _rel_bias(b_rel_bias[0]), bsz, seq)
    x3, _ = _matmul_res_ln(att, b_w_out[0].astype(BF16), x2, ln_mix_g[1], ln_mix_b[1])
    x4, _ = _moe_layer(x3, 1, moe_w_router[1], moe_router_bias[1], moe_w_gate_up, moe_w_down,
                       moe_w_shared_gate_up[1], moe_w_shared_down[1], ln_ffn_g[1], ln_ffn_b[1])
    return x4.reshape(bsz, seq, d)
```

```python
import jax
import jax.numpy as jnp
from jax import lax
from jax.experimental import pallas as pl
from jax.experimental.pallas import tpu as pltpu

F32 = jnp.float32
BF16 = jnp.bfloat16
HIGHEST = lax.Precision.HIGHEST

CHUNK = 64
DN_HEADS = 8
DN_DK = 128
CONV_WIDTH = 4
ATT_HEADS = 16
ATT_DH = 64
LEFT_CHUNKS = 8
MAX_REL = 256
N_EXPERTS = 256
TOP_K = 8
N_GROUPS = 8
TOPK_GROUPS = 4
ROUTED_SCALE = 2.5
DEPTH = 2
DN_ALPHA = (2 * DEPTH) ** 0.25
LN_EPS = 1e-5
NORM_EPS = 1e-6

LANES = 128
VMEM_LIMIT = 56 * 1024 * 1024

DN_TILE = 2 * CHUNK
ATT_TILE = 8 * CHUNK
ATT_BAND = (LEFT_CHUNKS + 2) * CHUNK
MOE_BLK = 256
DISPATCH_TILE = 256
DEST_TOK = LANES // TOP_K
DMA_UNROLL_TOK = 4
ROUTER_TILE = 512


def _cparams(sem):
    return pltpu.CompilerParams(dimension_semantics=sem, vmem_limit_bytes=VMEM_LIMIT)


def _sigmoid(x):
    return 1.0 / (1.0 + jnp.exp(-x))


def _dot(a, b):
    return jnp.dot(a, b, preferred_element_type=F32)


def _dot_nt(a, b, precision=None):
    return lax.dot_general(a, b, (((1,), (1,)), ((), ())), precision=precision,
                           preferred_element_type=F32)


def _dot_tn(a, b):
    return lax.dot_general(a, b, (((0,), (0,)), ((), ())), preferred_element_type=F32)


def _layer_norm(y, g, b):
    mu = jnp.mean(y, axis=-1, keepdims=True)
    d = y - mu
    var = jnp.mean(d * d, axis=-1, keepdims=True)
    return d * lax.rsqrt(var + LN_EPS) * g + b


def _mm_kernel(a_ref, w_ref, o_ref):
    o_ref[...] = _dot(a_ref[...], w_ref[...]).astype(o_ref.dtype)


def _matmul(a, w, out_dtype, tm, tn):
    m, k = a.shape
    n = w.shape[1]
    return pl.pallas_call(
        _mm_kernel,
        out_shape=jax.ShapeDtypeStruct((m, n), out_dtype),
        grid=(m // tm, n // tn),
        in_specs=[pl.BlockSpec((tm, k), lambda i, j: (i, 0)),
                  pl.BlockSpec((k, tn), lambda i, j: (0, j))],
        out_specs=pl.BlockSpec((tm, tn), lambda i, j: (i, j)),
        compiler_params=_cparams(("parallel", "arbitrary")),
        name="matmul",
    )(a, w)


def _mm_res_ln_kernel(a_ref, w_ref, x_ref, g_ref, b_ref, o_ref, ob_ref):
    y = DN_ALPHA * x_ref[...] + _dot(a_ref[...], w_ref[...])
    out = _layer_norm(y, g_ref[...], b_ref[...])
    o_ref[...] = out
    ob_ref[...] = out.astype(BF16)


def _matmul_res_ln(a, w, x, g, b, tm=512):
    m, k = a.shape
    d = w.shape[1]
    return pl.pallas_call(
        _mm_res_ln_kernel,
        out_shape=(jax.ShapeDtypeStruct((m, d), F32), jax.ShapeDtypeStruct((m, d), BF16)),
        grid=(m // tm,),
        in_specs=[pl.BlockSpec((tm, k), lambda i: (i, 0)),
                  pl.BlockSpec((k, d), lambda i: (0, 0)),
                  pl.BlockSpec((tm, d), lambda i: (i, 0)),
                  pl.BlockSpec((1, d), lambda i: (0, 0)),
                  pl.BlockSpec((1, d), lambda i: (0, 0))],
        out_specs=(pl.BlockSpec((tm, d), lambda i: (i, 0)),
                   pl.BlockSpec((tm, d), lambda i: (i, 0))),
        compiler_params=_cparams(("parallel",)),
        name="matmul_res_ln",
    )(a, w, x, g.reshape(1, d), b.reshape(1, d))


def _dn_kernel(qkv_ref, z_ref, ab_ref, cw_ref, alog_ref, dtb_ref, ong_ref, o_ref,
               xc_ref, s_ref):
    t = DN_TILE
    dk = DN_DK
    qk_w = DN_HEADS * dk
    c_idx = pl.program_id(1)

    @pl.when(c_idx == 0)
    def _():
        xc_ref[0:8, :] = jnp.zeros((8, xc_ref.shape[1]), F32)
        s_ref[...] = jnp.zeros(s_ref.shape, F32)

    xc_ref[8:8 + t, :] = qkv_ref[...]

    ab = ab_ref[...]
    a_sh = ab + dtb_ref[...]
    softplus = jnp.maximum(a_sh, 0.0) + jnp.log(1.0 + jnp.exp(-jnp.abs(a_sh)))
    g_full = -jnp.exp(alog_ref[...]) * softplus
    beta_full = _sigmoid(ab)

    row = lax.broadcasted_iota(jnp.int32, (t, t), 0)
    col = lax.broadcasted_iota(jnp.int32, (t, t), 1)
    same = (row // CHUNK) == (col // CHUNK)
    m_incl = same & (row >= col)
    m_strict = same & (row > col)
    l_incl = jnp.where(m_incl, 1.0, 0.0).astype(F32)
    l_all = jnp.where(same, 1.0, 0.0).astype(F32)
    gc_all = jnp.dot(l_incl, g_full, precision=HIGHEST, preferred_element_type=F32)
    gl_all = jnp.dot(l_all, g_full, precision=HIGHEST, preferred_element_type=F32)
    gc_t = gc_all.T

    def conv_act(off):
        acc = xc_ref[5:5 + t, off:off + dk] * cw_ref[0:1, off:off + dk]
        for j in range(1, CONV_WIDTH):
            acc = acc + xc_ref[5 + j:5 + j + t, off:off + dk] * cw_ref[j:j + 1, off:off + dk]
        return acc * _sigmoid(acc)

    hs = range(DN_HEADS)
    q, k, kb_, egc, gls, gcs = [], [], [], [], [], []
    a_b, p_mat, y = [], [], []
    for h in hs:
        qh = conv_act(h * dk)
        kh = conv_act(qk_w + h * dk)
        vh = conv_act(2 * qk_w + h * dk)
        qh = qh * lax.rsqrt(jnp.sum(qh * qh, axis=-1, keepdims=True) + NORM_EPS) * (dk ** -0.5)
        kh = kh * lax.rsqrt(jnp.sum(kh * kh, axis=-1, keepdims=True) + NORM_EPS)
        beta = beta_full[:, DN_HEADS + h:DN_HEADS + h + 1]
        gc = gc_all[:, h:h + 1]
        gr = gc_t[h:h + 1, :]
        decay = jnp.exp(jnp.where(m_incl, gc - gr, -jnp.inf))
        kb = kh * beta
        k_b = kh.astype(BF16)
        a_b.append(jnp.where(m_strict, _dot_nt(kb.astype(BF16), k_b) * decay, 0.0).astype(BF16))
        p_mat.append((_dot_nt(qh.astype(BF16), k_b) * decay).astype(BF16))
        e = jnp.exp(gc)
        y.append(jnp.concatenate([vh * beta, kb * e], axis=1))
        q.append(qh)
        k.append(kh)
        egc.append(e)
        gcs.append(gc)
        gls.append(gl_all[:, h:h + 1])

    y = [y[h] - _dot(a_b[h], y[h].astype(BF16)) for h in hs]
    for _ in range(5):
        a_b = [_dot(a_b[h], a_b[h]).astype(BF16) for h in hs]
        y = [y[h] + _dot(a_b[h], y[h].astype(BF16)) for h in hs]

    wq0, wq1, kd0, kd1 = [], [], [], []
    for h in hs:
        w = y[h][:, dk:]
        qg = q[h] * egc[h]
        kd = (k[h] * jnp.exp(gls[h] - gcs[h])).astype(BF16)
        wq0.append(jnp.concatenate([w[:CHUNK], qg[:CHUNK]], axis=0).astype(BF16))
        wq1.append(jnp.concatenate([w[CHUNK:], qg[CHUNK:]], axis=0).astype(BF16))
        kd0.append(kd[:CHUNK])
        kd1.append(kd[CHUNK:])

    s0 = [s_ref[h] for h in hs]
    r0 = [_dot(wq0[h], s0[h].astype(BF16)) for h in hs]
    vn0 = [y[h][:CHUNK, :dk] - r0[h][:CHUNK] for h in hs]
    s1 = [s0[h] * jnp.exp(gls[h][0:1, :]) + _dot_tn(kd0[h], vn0[h].astype(BF16)) for h in hs]
    r1 = [_dot(wq1[h], s1[h].astype(BF16)) for h in hs]
    vn1 = [y[h][CHUNK:, :dk] - r1[h][:CHUNK] for h in hs]
    for h in hs:
        s_ref[h] = (s1[h] * jnp.exp(gls[h][CHUNK:CHUNK + 1, :])
                    + _dot_tn(kd1[h], vn1[h].astype(BF16)))
    for h in hs:
        vn = jnp.concatenate([vn0[h], vn1[h]], axis=0).astype(BF16)
        o = jnp.concatenate([r0[h][CHUNK:], r1[h][CHUNK:]], axis=0) + _dot(p_mat[h], vn)
        o = o * lax.rsqrt(jnp.mean(o * o, axis=-1, keepdims=True) + NORM_EPS) * ong_ref[...]
        zh = z_ref[:, h * dk:(h + 1) * dk]
        o = o * (zh * _sigmoid(zh))
        o_ref[:, h * dk:(h + 1) * dk] = o.astype(o_ref.dtype)

    xc_ref[0:8, :] = xc_ref[t:t + 8, :]


def _deltanet(proj, ab, conv_w, a_log, dt_bias, out_norm_g, bsz, seq):
    n = proj.shape[0]
    t = DN_TILE
    nct = seq // t
    qkv_w = 3 * DN_HEADS * DN_DK
    v_w = DN_HEADS * DN_DK
    alog_p = jnp.zeros((1, LANES), F32).at[0, :DN_HEADS].set(a_log)
    dtb_p = jnp.zeros((1, LANES), F32).at[0, :DN_HEADS].set(dt_bias)
    return pl.pallas_call(
        _dn_kernel,
        out_shape=jax.ShapeDtypeStruct((n, v_w), BF16),
        grid=(bsz, nct),
        in_specs=[pl.BlockSpec((t, qkv_w), lambda b, c: (b * nct + c, 0)),
                  pl.BlockSpec((t, v_w), lambda b, c: (b * nct + c, qkv_w // v_w)),
                  pl.BlockSpec((t, LANES), lambda b, c: (b * nct + c, 0)),
                  pl.BlockSpec((CONV_WIDTH, qkv_w), lambda b, c: (0, 0)),
                  pl.BlockSpec((1, LANES), lambda b, c: (0, 0)),
                  pl.BlockSpec((1, LANES), lambda b, c: (0, 0)),
                  pl.BlockSpec((1, DN_DK), lambda b, c: (0, 0))],
        out_specs=pl.BlockSpec((t, v_w), lambda b, c: (b * nct + c, 0)),
        scratch_shapes=[pltpu.VMEM((t + 8, qkv_w), F32),
                        pltpu.VMEM((DN_HEADS, DN_DK, DN_DK), F32)],
        compiler_params=_cparams(("arbitrary", "arbitrary")),
        name="deltanet",
    )(proj, proj, ab, conv_w, alog_p, dtb_p, out_norm_g.reshape(1, DN_DK))


def _att_kernel(q_ref, kp_ref, kc_ref, vp_ref, vc_ref, bias_ref, o_ref, ks_ref, vs_ref):
    tq = ATT_TILE
    t_idx = pl.program_id(1)
    d = q_ref.shape[1]
    zeros = jnp.zeros((CHUNK, d), BF16)
    ks_ref[0:CHUNK, :] = zeros
    vs_ref[0:CHUNK, :] = zeros
    ks_ref[CHUNK:CHUNK + tq, :] = kp_ref[...]
    vs_ref[CHUNK:CHUNK + tq, :] = vp_ref[...]
    ks_ref[CHUNK + tq:CHUNK + 2 * tq, :] = kc_ref[...]
    vs_ref[CHUNK + tq:CHUNK + 2 * tq, :] = vc_ref[...]

    lane = lax.broadcasted_iota(jnp.int32, (CHUNK, 2 * ATT_DH), 1)
    first = lane < ATT_DH
    j_band = lax.broadcasted_iota(jnp.int32, (2 * CHUNK, ATT_BAND), 1)

    def chunk_body(c, carry):
        r0 = pl.multiple_of(c * CHUNK, CHUNK)
        kpos = t_idx * tq - (LEFT_CHUNKS + 1) * CHUNK + c * CHUNK + j_band
        valid = (j_band >= CHUNK) & (kpos >= 0)
        pairs = range(ATT_HEADS // 2)
        lanes = [slice(hp * 2 * ATT_DH, (hp + 1) * 2 * ATT_DH) for hp in pairs]
        scores = []
        for hp in pairs:
            qp = q_ref[pl.ds(r0, CHUNK), lanes[hp]] * jnp.asarray(ATT_DH ** -0.5, BF16)
            zq = jnp.zeros_like(qp)
            q2 = jnp.concatenate([jnp.where(first, qp, zq), jnp.where(first, zq, qp)], axis=0)
            s = _dot_nt(q2, ks_ref[pl.ds(r0, ATT_BAND), lanes[hp]])
            s = s + jnp.concatenate([bias_ref[2 * hp], bias_ref[2 * hp + 1]], axis=0)
            scores.append(jnp.where(valid, s, -jnp.inf))
        probs, denom = [], []
        for hp in pairs:
            m = jnp.max(scores[hp], axis=-1, keepdims=True)
            p = jnp.exp(scores[hp] - m)
            denom.append(jnp.sum(p, axis=-1, keepdims=True))
            probs.append(p.astype(BF16))
        for hp in pairs:
            r = _dot(probs[hp], vs_ref[pl.ds(r0, ATT_BAND), lanes[hp]]) / denom[hp]
            o_ref[pl.ds(r0, CHUNK), lanes[hp]] = jnp.where(first, r[:CHUNK], r[CHUNK:]).astype(o_ref.dtype)
        return carry

    lax.fori_loop(0, tq // CHUNK, chunk_body, 0)


def _attention(q, kv, bias, bsz, seq):
    n, d = q.shape
    tq = ATT_TILE
    nt = seq // tq
    rows = CHUNK + 2 * tq
    prev = lambda b, t: (b * nt + jnp.maximum(t - 1, 0), 0)
    cur = lambda b, t: (b * nt + t, 0)
    prev_v = lambda b, t: (b * nt + jnp.maximum(t - 1, 0), 1)
    cur_v = lambda b, t: (b * nt + t, 1)
    return pl.pallas_call(
        _att_kernel,
        out_shape=jax.ShapeDtypeStruct((n, d), BF16),
        grid=(bsz, nt),
        in_specs=[pl.BlockSpec((tq, d), cur),
                  pl.BlockSpec((tq, d), prev),
                  pl.BlockSpec((tq, d), cur),
                  pl.BlockSpec((tq, d), prev_v),
                  pl.BlockSpec((tq, d), cur_v),
                  pl.BlockSpec((ATT_HEADS, CHUNK, ATT_BAND), lambda b, t: (0, 0, 0))],
        out_specs=pl.BlockSpec((tq, d), cur),
        scratch_shapes=[pltpu.VMEM((rows, d), BF16), pltpu.VMEM((rows, d), BF16)],
        compiler_params=_cparams(("parallel", "arbitrary")),
        name="band_attention",
    )(q, kv, kv, kv, kv, bias)


def _rel_bias(table):
    max_idx = (CHUNK - 1) + (LEFT_CHUNKS + 1) * CHUNK + (CHUNK - 1)
    tail = jnp.broadcast_to(table[:, -1:], (table.shape[0], max_idx + 1 - table.shape[1]))
    rev = jnp.concatenate([table, tail], axis=1)[:, ::-1]
    rows = [rev[:, CHUNK - 1 - i:CHUNK - 1 - i + ATT_BAND] for i in range(CHUNK)]
    return jnp.stack(rows, axis=1).astype(F32)


def _router_kernel(x_ref, wt_ref, bias_ref, idx_ref, gate_ref, rank_ref, cnt_ref):
    tt = x_ref.shape[0]
    ne = wt_ref.shape[0]
    gsz = ne // N_GROUPS

    @pl.when(pl.program_id(0) == 0)
    def _():
        cnt_ref[...] = jnp.zeros(cnt_ref.shape, F32)

    logits = _dot_nt(wt_ref[...].astype(BF16), x_ref[...].astype(BF16))
    scores = _sigmoid(logits)
    sel = scores + bias_ref[...]
    neg = -jnp.inf

    iota_g = lax.broadcasted_iota(jnp.int32, (gsz, tt), 0)
    gs_rows = []
    for g in range(N_GROUPS):
        blk = sel[g * gsz:(g + 1) * gsz, :]
        m1 = jnp.max(blk, axis=0, keepdims=True)
        i1 = jnp.min(jnp.where(blk == m1, iota_g, gsz), axis=0, keepdims=True)
        m2 = jnp.max(jnp.where(iota_g == i1, neg, blk), axis=0, keepdims=True)
        gs_rows.append(m1 + m2)
    gsc = jnp.concatenate(gs_rows, axis=0)
    iota_n = lax.broadcasted_iota(jnp.int32, (N_GROUPS, tt), 0)
    chosen = jnp.zeros((N_GROUPS, tt), F32)
    for _ in range(TOPK_GROUPS):
        m = jnp.max(gsc, axis=0, keepdims=True)
        i = jnp.min(jnp.where(gsc == m, iota_n, N_GROUPS), axis=0, keepdims=True)
        hit = iota_n == i
        chosen = jnp.where(hit, 1.0, chosen)
        gsc = jnp.where(hit, neg, gsc)
    selm = jnp.concatenate(
        [jnp.where(chosen[g:g + 1, :] > 0.0, sel[g * gsz:(g + 1) * gsz, :], neg)
         for g in range(N_GROUPS)], axis=0)

    iota_e = lax.broadcasted_iota(jnp.int32, (ne, tt), 0)
    idx_rows, w_rows, hits = [], [], []
    for _ in range(TOP_K):
        m = jnp.max(selm, axis=0, keepdims=True)
        i = jnp.min(jnp.where(selm == m, iota_e, ne), axis=0, keepdims=True)
        hit = iota_e == i
        w_rows.append(jnp.sum(jnp.where(hit, scores, 0.0), axis=0, keepdims=True))
        idx_rows.append(i)
        hits.append(hit)
        selm = jnp.where(hit, neg, selm)
    wts = jnp.concatenate(w_rows, axis=0)
    wts = wts / jnp.sum(wts, axis=0, keepdims=True) * ROUTED_SCALE
    idx_ref[...] = jnp.concatenate(idx_rows, axis=0)
    gate_ref[...] = wts

    onehot = jnp.where(hits[0], 1.0, 0.0)
    for hit in hits[1:]:
        onehot = onehot + jnp.where(hit, 1.0, 0.0)
    onehot = onehot.astype(BF16)
    t_row = lax.broadcasted_iota(jnp.int32, (tt, tt), 0)
    t_col = lax.broadcasted_iota(jnp.int32, (tt, tt), 1)
    before = jnp.where(t_row < t_col, 1.0, 0.0).astype(BF16)
    pos = cnt_ref[:, 0:1] + _dot(onehot, before)
    rank_rows = [jnp.sum(jnp.where(hit, pos, 0.0), axis=0, keepdims=True) for hit in hits]
    rank_ref[...] = jnp.concatenate(rank_rows, axis=0).astype(jnp.int32)
    cnt_ref[...] = cnt_ref[...] + _dot(onehot, jnp.ones((tt, LANES), BF16))


def _router(x, w_router, router_bias):
    n, d = x.shape
    ne = w_router.shape[1]
    tt = ROUTER_TILE
    return pl.pallas_call(
        _router_kernel,
        out_shape=(jax.ShapeDtypeStruct((TOP_K, n), jnp.int32),
                   jax.ShapeDtypeStruct((TOP_K, n), F32),
                   jax.ShapeDtypeStruct((TOP_K, n), jnp.int32),
                   jax.ShapeDtypeStruct((ne, LANES), F32)),
        grid=(n // tt,),
        in_specs=[pl.BlockSpec((tt, d), lambda i: (i, 0)),
                  pl.BlockSpec((ne, d), lambda i: (0, 0)),
                  pl.BlockSpec((ne, 1), lambda i: (0, 0))],
        out_specs=(pl.BlockSpec((TOP_K, tt), lambda i: (0, i)),
                   pl.BlockSpec((TOP_K, tt), lambda i: (0, i)),
                   pl.BlockSpec((TOP_K, tt), lambda i: (0, i)),
                   pl.BlockSpec((ne, LANES), lambda i: (0, 0))),
        compiler_params=_cparams(("arbitrary",)),
        name="router",
    )(x, w_router.T, router_bias.reshape(ne, 1))


def _dest_kernel(idx_ref, rank_ref, pstart_ref, dest_ref):
    ne = pstart_ref.shape[0]
    tt = idx_ref.shape[1]
    iota_e = lax.broadcasted_iota(jnp.int32, (ne, tt), 0)
    pstart = pstart_ref[...]
    idx = idx_ref[...]
    rows = [jnp.sum(jnp.where(iota_e == idx[k:k + 1, :], pstart, 0), axis=0, keepdims=True)
            for k in range(TOP_K)]
    dest_ref[...] = jnp.concatenate(rows, axis=0) + rank_ref[...]


def _dest_rows(eidx_t, rank_t, pstart):
    k, n = eidx_t.shape
    ne = pstart.shape[0]
    tt = ROUTER_TILE
    dest_t = pl.pallas_call(
        _dest_kernel,
        out_shape=jax.ShapeDtypeStruct((k, n), jnp.int32),
        grid=(n // tt,),
        in_specs=[pl.BlockSpec((k, tt), lambda i: (0, i)),
                  pl.BlockSpec((k, tt), lambda i: (0, i)),
                  pl.BlockSpec((ne, 1), lambda i: (0, 0))],
        out_specs=pl.BlockSpec((k, tt), lambda i: (0, i)),
        compiler_params=_cparams(("parallel",)),
        name="dest_rows",
    )(eidx_t, rank_t, pstart.reshape(ne, 1))
    tiled = dest_t.reshape(k, n // DEST_TOK, DEST_TOK).transpose(1, 2, 0)
    return tiled.reshape(n // DISPATCH_TILE, DISPATCH_TILE // DEST_TOK, DEST_TOK * k)


def _for_each_row(tm, fn):
    for jg in range(DEST_TOK // DMA_UNROLL_TOK):
        def body(r, carry, jg=jg):
            t0 = pl.multiple_of(r * DEST_TOK, DEST_TOK)
            for j in range(jg * DMA_UNROLL_TOK, (jg + 1) * DMA_UNROLL_TOK):
                for k in range(TOP_K):
                    fn(t0 + j, k, (r, j * TOP_K + k))
            return carry

        lax.fori_loop(0, tm // DEST_TOK, body, 0)


def _dispatch_kernel(pend_ref, nv_ref, dest_ref, x_ref, xs_ref, zero_ref, sem, zsem):
    tm = x_ref.shape[0]
    ne = pend_ref.shape[0]
    nblk = xs_ref.shape[0] // MOE_BLK

    @pl.when(pl.program_id(0) == 0)
    def _():
        zero_ref[...] = jnp.zeros(zero_ref.shape, zero_ref.dtype)

        def zero_copy(row0):
            return pltpu.make_async_copy(
                zero_ref, xs_ref.at[pl.ds(pl.multiple_of(row0, MOE_BLK), MOE_BLK)], zsem)

        def has_rows(e):
            return pend_ref[e] > jnp.where(e > 0, pend_ref[jnp.maximum(e - 1, 0)], 0)

        def each_block(fn):
            def per_expert(e, carry):
                @pl.when(has_rows(e))
                def _():
                    fn(zero_copy(pend_ref[e] - MOE_BLK))
                return carry

            def per_tail(b, carry):
                fn(zero_copy(b * MOE_BLK))
                return carry

            lax.fori_loop(0, ne, per_expert, 0)
            lax.fori_loop(nv_ref[0], nblk, per_tail, 0)

        each_block(lambda cp: cp.start())
        each_block(lambda cp: cp.wait())

    def start_row(t, k, entry):
        pltpu.make_async_copy(x_ref.at[pl.ds(t, 1)],
                              xs_ref.at[pl.ds(dest_ref[0, entry[0], entry[1]], 1)], sem).start()

    _for_each_row(tm, start_row)
    for k in range(TOP_K):
        pltpu.make_async_copy(x_ref, xs_ref.at[pl.ds(0, tm)], sem).wait()


def _dispatch(x, dest3, pend, n_valid_blocks, p_rows):
    n, d = x.shape
    nt = dest3.shape[0]
    tm = DISPATCH_TILE
    grid_spec = pltpu.PrefetchScalarGridSpec(
        num_scalar_prefetch=2,
        grid=(nt,),
        in_specs=[pl.BlockSpec((1,) + dest3.shape[1:], lambda i, pe, nv: (i, 0, 0),
                               memory_space=pltpu.SMEM),
                  pl.BlockSpec((tm, d), lambda i, pe, nv: (i, 0))],
        out_specs=pl.BlockSpec(memory_space=pl.ANY),
        scratch_shapes=[pltpu.VMEM((MOE_BLK, d), x.dtype),
                        pltpu.SemaphoreType.DMA(()), pltpu.SemaphoreType.DMA(())],
    )
    return pl.pallas_call(
        _dispatch_kernel,
        out_shape=jax.ShapeDtypeStruct((p_rows, d), x.dtype),
        grid_spec=grid_spec,
        compiler_params=_cparams(("arbitrary",)),
        name="dispatch",
    )(pend, n_valid_blocks, dest3, x)


def _expert_kernel(be_ref, nv_ref, x_ref, wgu_ref, wd_ref, y_ref):
    i = pl.program_id(0)
    f = wd_ref.shape[2]

    @pl.when(i < nv_ref[0])
    def _():
        h = _dot(x_ref[...].astype(BF16), wgu_ref[0, 0].astype(BF16))
        g = h[:, :f]
        act = g * _sigmoid(g) * h[:, f:]
        y_ref[...] = _dot(act.astype(BF16), wd_ref[0, 0].astype(BF16))

    @pl.when(i >= nv_ref[0])
    def _():
        y_ref[...] = jnp.zeros(y_ref.shape, y_ref.dtype)


def _experts(xs, block_e, n_valid_blocks, w_gate_up, w_down, layer):
    p, d = xs.shape
    nblk = p // MOE_BLK
    f2 = w_gate_up.shape[3]
    f = w_down.shape[2]
    grid_spec = pltpu.PrefetchScalarGridSpec(
        num_scalar_prefetch=2,
        grid=(nblk,),
        in_specs=[pl.BlockSpec((MOE_BLK, d), lambda i, be, nv: (jnp.minimum(i, nv[0] - 1), 0)),
                  pl.BlockSpec((1, 1, d, f2), lambda i, be, nv: (layer, be[i], 0, 0)),
                  pl.BlockSpec((1, 1, f, d), lambda i, be, nv: (layer, be[i], 0, 0))],
        out_specs=pl.BlockSpec((MOE_BLK, d), lambda i, be, nv: (i, 0)),
    )
    return pl.pallas_call(
        _expert_kernel,
        out_shape=jax.ShapeDtypeStruct((p, d), F32),
        grid_spec=grid_spec,
        compiler_params=_cparams(("arbitrary",)),
        name="routed_experts",
    )(block_e, n_valid_blocks, xs, w_gate_up, w_down)


def _moe_final_kernel(dest_ref, x_ref, gate_ref, ys_ref, wsg_ref, wsd_ref, g_ref, b_ref,
                      o_ref, ob_ref, ybuf_ref, sem):
    tm = x_ref.shape[0]
    f = wsd_ref.shape[0]

    def start_row(t, k, entry):
        pltpu.make_async_copy(ys_ref.at[pl.ds(dest_ref[0, entry[0], entry[1]], 1)],
                              ybuf_ref.at[k, pl.ds(t, 1)], sem).start()

    _for_each_row(tm, start_row)
    x = x_ref[...]
    h = _dot(x.astype(BF16), wsg_ref[...])
    g = h[:, :f]
    acc = _dot((g * _sigmoid(g) * h[:, f:]).astype(BF16), wsd_ref[...])
    for k in range(TOP_K):
        pltpu.make_async_copy(ys_ref.at[pl.ds(0, tm)], ybuf_ref.at[k], sem).wait()
    gates = gate_ref[...]
    for k in range(TOP_K):
        acc = acc + ybuf_ref[k] * gates[:, k:k + 1]
    out = _layer_norm(DN_ALPHA * x + acc, g_ref[...], b_ref[...])
    o_ref[...] = out
    ob_ref[...] = out.astype(BF16)


def _moe_final(x, dest3, gates, ys, w_sgu, w_sd, g, b):
    n, d = x.shape
    nt = dest3.shape[0]
    tm = DISPATCH_TILE
    k = TOP_K
    return pl.pallas_call(
        _moe_final_kernel,
        out_shape=(jax.ShapeDtypeStruct((n, d), F32), jax.ShapeDtypeStruct((n, d), BF16)),
        grid=(nt,),
        in_specs=[pl.BlockSpec((1,) + dest3.shape[1:], lambda i: (i, 0, 0), memory_space=pltpu.SMEM),
                  pl.BlockSpec((tm, d), lambda i: (i, 0)),
                  pl.BlockSpec((tm, k), lambda i: (i, 0)),
                  pl.BlockSpec(memory_space=pl.ANY),
                  pl.BlockSpec(w_sgu.shape, lambda i: (0, 0)),
                  pl.BlockSpec(w_sd.shape, lambda i: (0, 0)),
                  pl.BlockSpec((1, d), lambda i: (0, 0)),
                  pl.BlockSpec((1, d), lambda i: (0, 0))],
        out_specs=(pl.BlockSpec((tm, d), lambda i: (i, 0)),
                   pl.BlockSpec((tm, d), lambda i: (i, 0))),
        scratch_shapes=[pltpu.VMEM((k, tm, d), ys.dtype), pltpu.SemaphoreType.DMA(())],
        compiler_params=_cparams(("arbitrary",)),
        name="moe_final",
    )(dest3, x, gates, ys, w_sgu, w_sd, g.reshape(1, d), b.reshape(1, d))


def _block_plan(counts, n_assign):
    ne = counts.shape[0]
    nblk = (n_assign + ne * MOE_BLK) // MOE_BLK
    padded = (counts + MOE_BLK - 1) // MOE_BLK * MOE_BLK
    pend = jnp.cumsum(padded)
    pstart = pend - padded
    n_valid = pend[-1] // MOE_BLK
    blk_start = jnp.minimum(jnp.arange(nblk, dtype=jnp.int32) * MOE_BLK, pend[-1] - 1)
    block_e = jnp.sum((pend[None, :] <= blk_start[:, None]).astype(jnp.int32), axis=1)
    block_e = jnp.minimum(block_e, ne - 1)
    return (pstart.astype(jnp.int32), pend.astype(jnp.int32), block_e,
            n_valid.astype(jnp.int32).reshape(1), nblk * MOE_BLK)


def _moe_layer(x_f32, layer, w_router, router_bias, w_gate_up, w_down, w_sgu, w_sd, ln_g, ln_b):
    eidx_t, gates_t, rank_t, cnt = _router(x_f32, w_router, router_bias)
    counts = cnt[:, 0].astype(jnp.int32)
    pstart, pend, block_e, n_valid, p_rows = _block_plan(counts, eidx_t.shape[0] * eidx_t.shape[1])
    dest3 = _dest_rows(eidx_t, rank_t, pstart)
    xs = _dispatch(x_f32, dest3, pend, n_valid, p_rows)
    ys = _experts(xs, block_e, n_valid, w_gate_up, w_down, layer)
    return _moe_final(x_f32, dest3, gates_t.T, ys, w_sgu.astype(BF16), w_sd.astype(BF16), ln_g, ln_b)


def kernel(x, a_w_in, a_conv_w, a_a_log, a_dt_bias, a_out_norm_g, a_w_out, w_kv_shared, b_w_q, b_rel_bias, b_w_out, moe_w_router, moe_router_bias, moe_w_gate_up, moe_w_down, moe_w_shared_gate_up, moe_w_shared_down, ln_mix_g, ln_mix_b, ln_ffn_g, ln_ffn_b):
    bsz, seq, d = x.shape
    n = bsz * seq
    xf = x.reshape(n, d)
    xb = xf.astype(BF16)

    main_w = 4 * DN_HEADS * DN_DK
    w_in = a_w_in[0]
    proj = _matmul(xb, w_in[:, :main_w].astype(BF16), F32, 512, 1024)
    w_ab = jnp.zeros((d, LANES), F32).at[:, :2 * DN_HEADS].set(w_in[:, main_w:]).astype(BF16)
    ab = _matmul(xb, w_ab, F32, 512, LANES)
    o = _deltanet(proj, ab, a_conv_w[0], a_a_log[0], a_dt_bias[0], a_out_norm_g[0], bsz, seq)
    x1, _ = _matmul_res_ln(o, a_w_out[0].astype(BF16), xf, ln_mix_g[0], ln_mix_b[0])
    x2, x2b = _moe_layer(x1, 0, moe_w_router[0], moe_router_bias[0], moe_w_gate_up, moe_w_down,
                         moe_w_shared_gate_up[0], moe_w_shared_down[0], ln_ffn_g[0], ln_ffn_b[0])

    kv = _matmul(x2b, w_kv_shared.astype(BF16), BF16, 512, 1024)
    q = _matmul(x2b, b_w_q[0].astype(BF16), BF16, 512, 1024)
    att = _attention(q, kv, _rel_bias(b_rel_bias[0]), bsz, seq)
    x3, _ = _matmul_res_ln(att, b_w_out[0].astype(BF16), x2, ln_mix_g[1], ln_mix_b[1])
    x4, _ = _moe_layer(x3, 1, moe_w_router[1], moe_router_bias[1], moe_w_gate_up, moe_w_down,
                       moe_w_shared_gate_up[1], moe_w_shared_down[1], ln_ffn_g[1], ln_ffn_b[1])
    return x4.reshape(bsz, seq, d)
```
